```python
import jax, jax.numpy as jnp
from jax import lax
import numpy as np

D_MODEL = 1024
BATCH = 16
SEQ = 256
DEPTH = 2
DEC_BATCH = 4
DEC_SEQ = 2048
PAST_LEN = 256

GRID_W = 64
EPS = 1e-6
BRANCH_W = D_MODEL
N_BRANCH = 3
RET_HEADS = 4
RET_DV = BRANCH_W // RET_HEADS
RET_DK = RET_DV // 2
RET_QK_W = RET_HEADS * RET_DK
RET_CHUNK = 128
LRU_W = BRANCH_W
LRU_BLOCKS = 8
LRU_BS = LRU_W // LRU_BLOCKS
LRU_C = 8.0
CONV_W = 4
ATT_HD = 64
ATT_Q_HEADS = BRANCH_W // ATT_HD
ATT_KV_HEADS = 4
ATT_GROUP = ATT_Q_HEADS // ATT_KV_HEADS
ATT_KV_W = ATT_KV_HEADS * ATT_HD
WINDOW = 128
Q_BLOCK = 128
ROPE_BASE = 10000.0
IN_SPLITS = (RET_QK_W, RET_QK_W, BRANCH_W, BRANCH_W,
             LRU_W, LRU_W,
             BRANCH_W, ATT_KV_W, ATT_KV_W, BRANCH_W)
IN_W = sum(IN_SPLITS)

kernel_name = "hybrid_retention_rglru_swa_diffusion_step"

F32 = jnp.float32


def rms_norm(x, g):
    xf = x.astype(F32)
    y = xf * lax.rsqrt(jnp.mean(xf * xf, axis=-1, keepdims=True) + EPS)
    return (y * g.astype(F32)).astype(x.dtype)


def head_norm(x):
    mu = jnp.mean(x, axis=-1, keepdims=True)
    xc = x - mu
    return xc * lax.rsqrt(jnp.mean(xc * xc, axis=-1, keepdims=True) + EPS)


def axial_rope(n_tokens, dim):
    rows = n_tokens // GRID_W
    r = jnp.repeat(jnp.arange(rows, dtype=F32), GRID_W)
    col = jnp.tile(jnp.arange(GRID_W, dtype=F32), rows)
    nq = dim // 4
    inv = ROPE_BASE ** (-jnp.arange(nq, dtype=F32) / nq)
    ang = jnp.stack([r[:, None] * inv, col[:, None] * inv], axis=1)
    return jnp.cos(ang), jnp.sin(ang)


def apply_rope(x, cos, sin):
    B, n, H, dim = x.shape
    xr = x.astype(F32).reshape(B, n, H, 2, 2, dim // 4)
    x1, x2 = xr[..., 0, :], xr[..., 1, :]
    c = cos[None, :, None]
    s = sin[None, :, None]
    out = jnp.stack([x1 * c - x2 * s, x2 * c + x1 * s], axis=-2)
    return out.reshape(B, n, H, dim).astype(x.dtype)


def retention_dir(q, k, v, log_g, s0):
    B, L, H, DK = q.shape
    DV = v.shape[-1]
    n = L // RET_CHUNK

    def chunks(t):
        return t.reshape(B, n, RET_CHUNK, H, t.shape[-1]).transpose(1, 0, 3, 2, 4)

    idx = jnp.arange(RET_CHUNK, dtype=F32)
    diff = idx[:, None] - idx[None, :]
    lg = log_g.astype(F32)
    decay_in = jnp.where(diff >= 0, jnp.exp(lg[:, None, None] * jnp.maximum(diff, 0.0)), 0.0)
    decay_q = jnp.exp(lg[:, None] * (idx + 1.0))[:, :, None]
    decay_k = jnp.exp(lg[:, None] * (RET_CHUNK - 1.0 - idx))[:, :, None]
    decay_c = jnp.exp(lg * RET_CHUNK)[:, None, None]

    def step(s, blk):
        qb, kb, vb = blk
        a = jnp.einsum('bhqd,bhkd->bhqk', qb, kb) * decay_in
        o = (jnp.einsum('bhqk,bhkv->bhqv', a, vb)
             + jnp.einsum('bhqd,bhdv->bhqv', qb * decay_q, s))
        s = s * decay_c + jnp.einsum('bhkd,bhkv->bhdv', kb * decay_k, vb)
        return s, o

    s_fin, o = lax.scan(step, s0.astype(F32), (chunks(q), chunks(k), chunks(v)))
    o = o.transpose(1, 0, 3, 2, 4).reshape(B, L, H, DV)
    return o, s_fin


def bi_retention(q, k, v, log_g, s0):
    of, sf = retention_dir(q, k, v, log_g[0], s0[:, 0])
    ob, sb = retention_dir(q[:, ::-1], k[:, ::-1], v[:, ::-1], log_g[1], s0[:, 1])
    return of + ob[:, ::-1], jnp.stack([sf, sb], axis=1)


def depthwise_conv(x, w, b):
    y = lax.conv_general_dilated(x, w.astype(x.dtype)[:, None, :], window_strides=(1,),
                                 padding=[(1, 2)], dimension_numbers=('NWC', 'WIO', 'NWC'),
                                 feature_group_count=x.shape[-1])
    return y + b.astype(x.dtype)


def _lin_combine(e1, e2):
    a1, b1 = e1
    a2, b2 = e2
    return a1 * a2, a2 * b1 + b2


def rglru_dir(x, w_a, b_a, w_x, b_x, lam, h0):
    B, L, W = x.shape
    xb = x.reshape(B, L, LRU_BLOCKS, LRU_BS)
    r = jax.nn.sigmoid(jnp.einsum('blnc,ncd->blnd', xb, w_a.astype(F32)).reshape(B, L, W) + b_a.astype(F32))
    i = jax.nn.sigmoid(jnp.einsum('blnc,ncd->blnd', xb, w_x.astype(F32)).reshape(B, L, W) + b_x.astype(F32))
    log_a = -LRU_C * r * jax.nn.softplus(-lam.astype(F32))
    a = jnp.exp(log_a)
    u = jnp.sqrt(-jnp.expm1(2.0 * log_a)) * (i * x)
    a_cum, h = lax.associative_scan(_lin_combine, (a, u), axis=1)
    h = h + a_cum * h0.astype(F32)[:, None, :]
    return h, h[:, -1]


def sink_attend(qb, kb, vb, sink, mask):
    s = jnp.einsum('bqhgd,bkhd->bhgqk', qb.astype(F32), kb.astype(F32)) * (ATT_HD ** -0.5)
    if mask is not None:
        s = jnp.where(mask, s, -jnp.inf)
    sk = sink.astype(F32)[None, :, :, None]
    m = jnp.maximum(jnp.max(s, axis=-1), sk)
    p = jnp.exp(s - m[..., None])
    denom = jnp.sum(p, axis=-1) + jnp.exp(sk - m)
    return jnp.einsum('bhgqk,bkhd->bqhgd', p / denom[..., None], vb.astype(F32))


def context_attention(q, k, v, sink):
    B, P = q.shape[:2]
    qg = q.reshape(B, P, ATT_KV_HEADS, ATT_GROUP, ATT_HD)
    sk = sink.reshape(ATT_KV_HEADS, ATT_GROUP)

    def blk(i):
        qb = lax.dynamic_slice_in_dim(qg, i * Q_BLOCK, Q_BLOCK, axis=1)
        return sink_attend(qb, k, v, sk, None)

    o = lax.map(blk, jnp.arange(P // Q_BLOCK))
    return o.transpose(1, 0, 2, 3, 4, 5).reshape(B, P, ATT_Q_HEADS * ATT_HD)


def latent_attention(q, k, v, ck, cv, sink):
    B, L = q.shape[:2]
    qg = q.reshape(B, L, ATT_KV_HEADS, ATT_GROUP, ATT_HD)
    sk = sink.reshape(ATT_KV_HEADS, ATT_GROUP)
    pad = ((0, 0), (WINDOW, WINDOW), (0, 0), (0, 0))
    kp = jnp.pad(k, pad)
    vp = jnp.pad(v, pad)
    span = Q_BLOCK + 2 * WINDOW
    ctx_mask = jnp.ones((Q_BLOCK, ck.shape[1]), dtype=bool)
    ckf = ck.astype(F32)
    cvf = cv.astype(F32)

    def blk(i):
        start = i * Q_BLOCK
        qb = lax.dynamic_slice_in_dim(qg, start, Q_BLOCK, axis=1)
        kb = lax.dynamic_slice_in_dim(kp, start, span, axis=1).astype(F32)
        vb = lax.dynamic_slice_in_dim(vp, start, span, axis=1).astype(F32)
        qpos = start + jnp.arange(Q_BLOCK)
        kpos = start - WINDOW + jnp.arange(span)
        band = ((jnp.abs(qpos[:, None] - kpos[None, :]) <= WINDOW)
                & (kpos >= 0)[None, :] & (kpos < L)[None, :])
        mask = jnp.concatenate([band, ctx_mask], axis=1)
        return sink_attend(qb, jnp.concatenate([kb, ckf], axis=1),
                           jnp.concatenate([vb, cvf], axis=1), sk, mask)

    o = lax.map(blk, jnp.arange(L // Q_BLOCK))
    return o.transpose(1, 0, 2, 3, 4, 5).reshape(B, L, ATT_Q_HEADS * ATT_HD)


def trunk_layer(x, cond, lp, ctx):
    B, L, _ = x.shape
    dt = x.dtype
    mod = jnp.dot(jax.nn.silu(cond), lp['w_ada']) + lp['b_ada']
    shift, scale, gate = jnp.split(mod[:, None, :], 3, axis=-1)
    h = rms_norm(x, lp['norm_pre']) * (1.0 + scale) + shift
    offs = np.cumsum(IN_SPLITS)[:-1].tolist()
    rq, rk, rv, rg, lx, lgt, aq, ak, av, ag = jnp.split(jnp.dot(h, lp['w_in']), offs, axis=-1)
    rq = rq.reshape(B, L, RET_HEADS, RET_DK).astype(F32) * (RET_DK ** -0.5)
    rk = rk.reshape(B, L, RET_HEADS, RET_DK).astype(F32)
    rv = rv.reshape(B, L, RET_HEADS, RET_DV).astype(F32)
    aq = aq.reshape(B, L, ATT_Q_HEADS, ATT_HD)
    ak = ak.reshape(B, L, ATT_KV_HEADS, ATT_HD)
    av = av.reshape(B, L, ATT_KV_HEADS, ATT_HD)
    if ctx is None:
        ret_s0 = jnp.zeros((B, 2, RET_HEADS, RET_DK, RET_DV), F32)
        lru_h0 = jnp.zeros((B, 2, LRU_W), F32)
    else:
        ret_s0, lru_h0, ck, cv = ctx
        cos_r, sin_r = axial_rope(L, RET_DK)
        rq = apply_rope(rq, cos_r, sin_r)
        rk = apply_rope(rk, cos_r, sin_r)
        cos_a, sin_a = axial_rope(L, ATT_HD)
        aq = apply_rope(aq, cos_a, sin_a)
        ak = apply_rope(ak, cos_a, sin_a)
    log_g = -jax.nn.softplus(-lp['ret_decay'].astype(F32))
    ro, ret_s = bi_retention(rq, rk, rv, log_g, ret_s0)
    ro = head_norm(ro).reshape(B, L, BRANCH_W) * jax.nn.silu(rg.astype(F32))
    lxc = depthwise_conv(lx, lp['conv_w'], lp['conv_b']).astype(F32)
    hf, lf = rglru_dir(lxc, lp['lru_wa'][0], lp['lru_ba'][0], lp['lru_wx'][0], lp['lru_bx'][0],
                       lp['lru_lambda'][0], lru_h0[:, 0])
    hb, lb = rglru_dir(lxc[:, ::-1], lp['lru_wa'][1], lp['lru_ba'][1], lp['lru_wx'][1], lp['lru_bx'][1],
                       lp['lru_lambda'][1], lru_h0[:, 1])
    lo = (hf + hb[:, ::-1]) * jax.nn.silu(lgt.astype(F32))
    lru_s = jnp.stack([lf, lb], axis=1)
    if ctx is None:
        ao = context_attention(aq, ak, av, lp['att_sink'])
    else:
        ao = latent_attention(aq, ak, av, ck, cv, lp['att_sink'])
    ao = ao * jax.nn.silu(ag.astype(F32))
    branches = jnp.stack([ro, lo, ao], axis=2).astype(dt)
    z = jnp.einsum('blnc,ncd->blnd', branches, lp['w_branch'])
    g = jax.nn.sigmoid(jnp.dot(h, lp['w_merge']) + lp['b_merge']).reshape(B, L, N_BRANCH, D_MODEL)
    merged = jnp.sum(g * z, axis=2)
    out = jnp.dot(merged, lp['w_out'])
    x = x + gate * rms_norm(out, lp['norm_post'])
    if ctx is None:
        return x, (ret_s, lru_s, ak, av)
    return x


def setup_inputs(seed: int = 0) -> dict:
    key = jax.random.key(seed)
    ks = jax.random.split(key, 32)
    D = D_MODEL

    def nrm(k, shape, s):
        return jax.random.normal(k, shape, F32) * s

    gamma0 = 1.0 - 2.0 ** (-5.0 - jnp.arange(RET_HEADS, dtype=F32))
    a0 = jax.random.uniform(ks[20], (DEPTH, 2, LRU_W), F32, 0.9, 0.999)
    root = a0 ** (1.0 / LRU_C)
    return {
        "x_prompt": nrm(ks[0], (BATCH, SEQ, D), 1.0),
        "x_sample": nrm(ks[1], (DEC_BATCH, DEC_SEQ, D), 1.0),
        "cache_k": nrm(ks[2], (DEC_BATCH, DEPTH, PAST_LEN, ATT_KV_HEADS, ATT_HD), 1.0),
        "cache_v": nrm(ks[3], (DEC_BATCH, DEPTH, PAST_LEN, ATT_KV_HEADS, ATT_HD), 1.0),
        "state_ret": nrm(ks[4], (DEC_BATCH, DEPTH, 2, RET_HEADS, RET_DK, RET_DV), 1.0),
        "state_lru": nrm(ks[5], (DEC_BATCH, DEPTH, 2, LRU_W), 1.0),
        "c": nrm(ks[6], (DEC_BATCH, D), 1.0),
        "c_ctx": nrm(ks[7], (D,), 1.0),
        "w_ada": nrm(ks[8], (DEPTH, D, 3 * D), D ** -0.5),
        "b_ada": nrm(ks[9], (DEPTH, 3 * D), 0.02),
        "norm_pre": 1.0 + nrm(ks[10], (DEPTH, D), 0.05),
        "norm_post": 1.0 + nrm(ks[11], (DEPTH, D), 0.05),
        "w_in": nrm(ks[12], (DEPTH, D, IN_W), D ** -0.5),
        "ret_decay": jnp.log(gamma0 / (1.0 - gamma0)) + nrm(ks[13], (DEPTH, 2, RET_HEADS), 0.1),
        "conv_w": nrm(ks[14], (DEPTH, CONV_W, LRU_W), CONV_W ** -0.5),
        "conv_b": nrm(ks[15], (DEPTH, LRU_W), 0.02),
        "lru_wa": nrm(ks[16], (DEPTH, 2, LRU_BLOCKS, LRU_BS, LRU_BS), LRU_BS ** -0.5),
        "lru_ba": nrm(ks[17], (DEPTH, 2, LRU_W), 0.02),
        "lru_wx": nrm(ks[18], (DEPTH, 2, LRU_BLOCKS, LRU_BS, LRU_BS), LRU_BS ** -0.5),
        "lru_bx": nrm(ks[19], (DEPTH, 2, LRU_W), 0.02),
        "lru_lambda": jnp.log(root) - jnp.log1p(-root),
        "att_sink": nrm(ks[21], (DEPTH, ATT_Q_HEADS), 0.5),
        "w_branch": nrm(ks[22], (DEPTH, N_BRANCH, BRANCH_W, D), BRANCH_W ** -0.5),
        "w_merge": nrm(ks[23], (DEPTH, D, N_BRANCH * D), D ** -0.5),
        "b_merge": nrm(ks[24], (DEPTH, N_BRANCH * D), 0.02),
        "w_out": nrm(ks[25], (DEPTH, D, D), D ** -0.5),
    }


def reference(x_prompt, x_sample, cache_k, cache_v, state_ret, state_lru, c, c_ctx,
              w_ada, b_ada, norm_pre, norm_post, w_in, ret_decay, conv_w, conv_b,
              lru_wa, lru_ba, lru_wx, lru_bx, lru_lambda, att_sink, w_branch, w_merge,
              b_merge, w_out):
    yp = x_prompt
    ys = x_sample
    cond_ctx = c_ctx[None, :]
    new_k, new_v, new_ret, new_lru = [], [], [], []
    for l in range(DEPTH):
        lp = {
            'w_ada': w_ada[l], 'b_ada': b_ada[l], 'norm_pre': norm_pre[l], 'norm_post': norm_post[l],
            'w_in': w_in[l], 'ret_decay': ret_decay[l], 'conv_w': conv_w[l], 'conv_b': conv_b[l],
            'lru_wa': lru_wa[l], 'lru_ba': lru_ba[l], 'lru_wx': lru_wx[l], 'lru_bx': lru_bx[l],
            'lru_lambda': lru_lambda[l], 'att_sink': att_sink[l], 'w_branch': w_branch[l],
            'w_merge': w_merge[l], 'b_merge': b_merge[l], 'w_out': w_out[l],
        }
        yp, (r_s, l_s, k_l, v_l) = trunk_layer(yp, cond_ctx, lp, None)
        new_ret.append(r_s.astype(x_prompt.dtype))
        new_lru.append(l_s.astype(x_prompt.dtype))
        new_k.append(k_l)
        new_v.append(v_l)
        ys = trunk_layer(ys, c, lp, (state_ret[:, l], state_lru[:, l], cache_k[:, l], cache_v[:, l]))
    new_cache_k = jnp.stack(new_k, axis=1)
    new_cache_v = jnp.stack(new_v, axis=1)
    new_state_ret = jnp.stack(new_ret, axis=1)
    new_state_lru = jnp.stack(new_lru, axis=1)
    return (yp, ys, new_cache_k, new_cache_v, new_state_ret, new_state_lru)
```

```python
import functools

import numpy as np
import jax
import jax.numpy as jnp
from jax import lax
from jax.experimental import pallas as pl
from jax.experimental.pallas import tpu as pltpu

F32 = jnp.float32
BF16 = jnp.bfloat16

D = 1024
DEPTH = 2
GRID_W = 64
EPS = 1e-6
N_BRANCH = 3
RET_HEADS = 4
RET_DV = 256
RET_DK = 128
CHUNK = 128
LRU_BLOCKS = 8
LRU_BS = 128
LRU_C = 8.0
ATT_HD = 64
ATT_Q_HEADS = 16
ATT_KV_HEADS = 4
ATT_GROUP = 4
ATT_KV_W = 256
WINDOW = 128
ROPE_BASE = 10000.0

OFF_GM = 0
OFF_AQ = 3072
OFF_AG = 4096
OFF_RV = 5120
OFF_RG = 6144
OFF_LX = 7168
OFF_LG = 8192
OFF_RQ = 9216
OFF_RK = 9728
OFF_AK = 10240
OFF_AV = 10496
W_ALL = 10752

VMEM_LIMIT = 48 * 1024 * 1024


def _cparams(sem):
    return pltpu.CompilerParams(dimension_semantics=sem, vmem_limit_bytes=VMEM_LIMIT)


def _sigmoid(x):
    return jax.nn.sigmoid(x)


def _silu(x):
    return x * jax.nn.sigmoid(x)


def _softplus(z):
    return jnp.maximum(z, 0.0) + jnp.log1p(jnp.exp(-jnp.abs(z)))


def _dot(a, b):
    return jnp.dot(a, b, preferred_element_type=F32)


def _dot_nt(a, b):
    return lax.dot_general(a, b, (((1,), (1,)), ((), ())), preferred_element_type=F32)


def _dot_tn(a, b):
    return lax.dot_general(a, b, (((0,), (0,)), ((), ())), preferred_element_type=F32)


def _ada_kernel(c_ref, w_ref, b_ref, o_ref):
    s = _silu(c_ref[...])
    o_ref[...] = _dot(s.astype(BF16), w_ref[...].astype(BF16)) + b_ref[...]


def _ada_mod(cond8, w_ada, b_ada):
    tn = 1024
    return pl.pallas_call(
        _ada_kernel,
        grid=(DEPTH, 3 * D // tn),
        in_specs=[
            pl.BlockSpec((8, D), lambda l, j: (0, 0)),
            pl.BlockSpec((None, D, tn), lambda l, j: (l, 0, j)),
            pl.BlockSpec((None, 1, tn), lambda l, j: (l, 0, j)),
        ],
        out_specs=pl.BlockSpec((None, 8, tn), lambda l, j: (l, 0, j)),
        out_shape=jax.ShapeDtypeStruct((DEPTH, 8, 3 * D), F32),
        compiler_params=_cparams(("arbitrary", "arbitrary")),
    )(cond8, w_ada, b_ada.reshape(DEPTH, 1, 3 * D))


IN_TM = 512
IN_TN = 512
N_GATE_TILES = (N_BRANCH * D) // IN_TN


def _in_kernel(x_ref, mod_ref, g_ref, w_ref, b_ref, o_ref, h_scr):
    j = pl.program_id(1)

    @pl.when(j == 0)
    def _():
        x = x_ref[...]
        y = x * lax.rsqrt(jnp.mean(x * x, axis=-1, keepdims=True) + EPS) * g_ref[...]
        h = y * (1.0 + mod_ref[:, D:2 * D]) + mod_ref[:, 0:D]
        h_scr[...] = h.astype(BF16)

    acc = _dot(h_scr[...], w_ref[...]) + b_ref[...]

    @pl.when(j < N_GATE_TILES)
    def _():
        o_ref[...] = _sigmoid(acc)

    @pl.when(j >= N_GATE_TILES)
    def _():
        o_ref[...] = acc


def _in_proj(x2d, mod3, row_fn, g_pre, w_all, b_all):
    T = x2d.shape[0]
    return pl.pallas_call(
        _in_kernel,
        grid=(T // IN_TM, W_ALL // IN_TN),
        in_specs=[
            pl.BlockSpec((IN_TM, D), lambda i, j: (i, 0)),
            pl.BlockSpec((None, 1, 3 * D), lambda i, j: (row_fn(i), 0, 0)),
            pl.BlockSpec((1, D), lambda i, j: (0, 0)),
            pl.BlockSpec((D, IN_TN), lambda i, j: (0, j)),
            pl.BlockSpec((1, IN_TN), lambda i, j: (0, j)),
        ],
        out_specs=pl.BlockSpec((IN_TM, IN_TN), lambda i, j: (i, j)),
        out_shape=jax.ShapeDtypeStruct((T, W_ALL), F32),
        scratch_shapes=[pltpu.VMEM((IN_TM, D), BF16)],
        compiler_params=_cparams(("arbitrary", "arbitrary")),
    )(x2d, mod3, g_pre, w_all, b_all)


def _rope(x, cos_t, sin_t, half):
    lane = lax.broadcasted_iota(jnp.int32, x.shape, 1)
    up = pltpu.roll(x, x.shape[1] - half, axis=1)
    dn = pltpu.roll(x, half, axis=1)
    partner = jnp.where((lane & half) == 0, up, dn)
    return x * cos_t + partner * sin_t


def _rope_tables(n_tokens, dim):
    nq = dim // 4
    lane = np.arange(128)
    within = lane % dim
    axis = within // (2 * nq)
    freq = within % nq
    sign = np.where((within % (2 * nq)) < nq, -1.0, 1.0).astype(np.float32)
    inv = ROPE_BASE ** (-jnp.arange(nq, dtype=F32) / nq)
    t = jnp.arange(n_tokens)
    pos = jnp.stack([(t // GRID_W).astype(F32), (t % GRID_W).astype(F32)], axis=1)
    ang = pos[:, axis] * inv[freq][None, :]
    return jnp.cos(ang), jnp.sin(ang) * sign[None, :]


def _ret_kernel(*refs, n_chunks, has_ctx):
    if has_ctx:
        (q_ref, k_ref, v_ref, g_ref, rd_ref, cos_ref, sin_ref, s0_ref,
         o_ref, s_ref, qs, ks, sb_all, sf_run, sb_run) = refs
    else:
        (q_ref, k_ref, v_ref, g_ref, rd_ref,
         o_ref, s_ref, qs, ks, sb_all, sf_run, sb_run) = refs
    C = CHUNK

    q = q_ref[...] * (RET_DK ** -0.5)
    k = k_ref[...]
    if has_ctx:
        q = _rope(q, cos_ref[...], sin_ref[...], 32)
        k = _rope(k, cos_ref[...], sin_ref[...], 32)
        sf_run[...] = s0_ref[0]
        sb_run[...] = s0_ref[1]
    else:
        sf_run[...] = jnp.zeros_like(sf_run)
        sb_run[...] = jnp.zeros_like(sb_run)
    qs[...] = q
    ks[...] = k

    lg_f = -_softplus(-rd_ref[0])
    lg_b = -_softplus(-rd_ref[1])
    lgf = lg_f[:, :C]
    lgb = lg_b[:, :C]
    ii = lax.broadcasted_iota(jnp.int32, (C, C), 0)
    jj = lax.broadcasted_iota(jnp.int32, (C, C), 1)
    diff = (ii - jj).astype(F32)
    dmat = (jnp.where(diff >= 0, jnp.exp(lgf * jnp.maximum(diff, 0.0)), 0.0)
            + jnp.where(diff <= 0, jnp.exp(lgb * jnp.maximum(-diff, 0.0)), 0.0))
    ri = ii.astype(F32)
    dq_f = jnp.exp(lgf * (ri + 1.0))
    dk_f = jnp.exp(lgf * (C - 1.0 - ri))
    dq_b = jnp.exp(lgb * (C - ri))
    dk_b = jnp.exp(lgb * ri)
    dc_f = jnp.exp(lg_f * float(C))
    dc_b = jnp.exp(lg_b * float(C))

    def bwd_step(t, carry):
        c = n_chunks - 1 - t
        r0 = pl.multiple_of(c * C, C)
        sb = sb_run[...]
        sb_all[c] = sb
        kd = (ks[pl.ds(r0, C), :] * dk_b).astype(BF16)
        vv = v_ref[pl.ds(r0, C), :].astype(BF16)
        sb_run[...] = sb * dc_b + _dot_tn(kd, vv)
        return carry

    def fwd_step(c, carry):
        r0 = pl.multiple_of(c * C, C)
        qc = qs[pl.ds(r0, C), :]
        kc = ks[pl.ds(r0, C), :]
        vc = v_ref[pl.ds(r0, C), :].astype(BF16)
        a = _dot_nt(qc.astype(BF16), kc.astype(BF16)) * dmat
        sf = sf_run[...]
        o = (_dot(a.astype(BF16), vc)
             + _dot((qc * dq_f).astype(BF16), sf.astype(BF16))
             + _dot((qc * dq_b).astype(BF16), sb_all[c].astype(BF16)))
        sf_run[...] = sf * dc_f + _dot_tn((kc * dk_f).astype(BF16), vc)
        mu = jnp.mean(o, axis=-1, keepdims=True)
        oc = o - mu
        on = oc * lax.rsqrt(jnp.mean(oc * oc, axis=-1, keepdims=True) + EPS)
        o_ref[pl.ds(r0, C), :] = on * _silu(g_ref[pl.ds(r0, C), :])
        return carry

    if n_chunks <= 2:
        for t in range(n_chunks):
            bwd_step(t, 0)
        for c in range(n_chunks):
            fwd_step(c, 0)
    else:
        lax.fori_loop(0, n_chunks, bwd_step, 0)
        lax.fori_loop(0, n_chunks, fwd_step, 0)
    s_ref[0] = sf_run[...]
    s_ref[1] = sb_run[...]


def _retention(pm3, rd4, ropes, s0):
    B, L, _ = pm3.shape
    n_chunks = L // CHUNK
    has_ctx = s0 is not None
    H = RET_HEADS
    in_specs = [
        pl.BlockSpec((None, L, RET_DK), lambda b, h: (b, 0, OFF_RQ // RET_DK + h)),
        pl.BlockSpec((None, L, RET_DK), lambda b, h: (b, 0, OFF_RK // RET_DK + h)),
        pl.BlockSpec((None, L, RET_DV), lambda b, h: (b, 0, OFF_RV // RET_DV + h)),
        pl.BlockSpec((None, L, RET_DV), lambda b, h: (b, 0, OFF_RG // RET_DV + h)),
        pl.BlockSpec((2, None, 1, RET_DV), lambda b, h: (0, h, 0, 0)),
    ]
    args = [pm3, pm3, pm3, pm3, rd4]
    if has_ctx:
        in_specs += [
            pl.BlockSpec((L, 128), lambda b, h: (0, 0)),
            pl.BlockSpec((L, 128), lambda b, h: (0, 0)),
            pl.BlockSpec((None, 2, None, RET_DK, RET_DV), lambda b, h: (b, 0, h, 0, 0)),
        ]
        args += [ropes[0], ropes[1], s0]
    return pl.pallas_call(
        functools.partial(_ret_kernel, n_chunks=n_chunks, has_ctx=has_ctx),
        grid=(B, H),
        in_specs=in_specs,
        out_specs=[
            pl.BlockSpec((None, L, RET_DV), lambda b, h: (b, 0, h)),
            pl.BlockSpec((None, 2, None, RET_DK, RET_DV), lambda b, h: (b, 0, h, 0, 0)),
        ],
        out_shape=[
            jax.ShapeDtypeStruct((B, L, D), F32),
            jax.ShapeDtypeStruct((B, 2, H, RET_DK, RET_DV), F32),
        ],
        scratch_shapes=[
            pltpu.VMEM((L, RET_DK), F32),
            pltpu.VMEM((L, RET_DK), F32),
            pltpu.VMEM((n_chunks, RET_DK, RET_DV), F32),
            pltpu.VMEM((RET_DK, RET_DV), F32),
            pltpu.VMEM((RET_DK, RET_DV), F32),
        ],
        compiler_params=_cparams(("arbitrary", "arbitrary")),
    )(*args)


def _lru_kernel(*refs, L, has_ctx):
    if has_ctx:
        (x_ref, gt_ref, cw_ref, cb_ref, w_ref, b_ref, lam_ref, h0_ref,
         o_ref, s_ref, xc_scr, hf_scr, ab_scr, ub_scr) = refs
    else:
        (x_ref, gt_ref, cw_ref, cb_ref, w_ref, b_ref, lam_ref,
         o_ref, s_ref, xc_scr, hf_scr, ab_scr, ub_scr) = refs
    R = CHUNK
    n = L // R
    W = LRU_BS

    x = x_ref[...]
    row = lax.broadcasted_iota(jnp.int32, (L, W), 0)
    xm1 = jnp.where(row >= 1, pltpu.roll(x, 1, axis=0), 0.0)
    xp1 = jnp.where(row < L - 1, pltpu.roll(x, L - 1, axis=0), 0.0)
    xp2 = jnp.where(row < L - 2, pltpu.roll(x, L - 2, axis=0), 0.0)
    xc_scr[...] = (cw_ref[0:1, :] * xm1 + cw_ref[1:2, :] * x + cw_ref[2:3, :] * xp1
                   + cw_ref[3:4, :] * xp2 + cb_ref[...])

    sp_f = _softplus(-lam_ref[0:1, :])
    sp_b = _softplus(-lam_ref[1:2, :])
    rowm = lax.broadcasted_iota(jnp.int32, (R, W), 0) & 7

    def gates(zr, zi, sp, xc):
        r = _sigmoid(zr)
        i = _sigmoid(zi)
        log_a = -LRU_C * r * sp
        a = jnp.exp(log_a)
        u = jnp.sqrt(-jnp.tanh(log_a) * (a * a + 1.0)) * (i * xc)
        return a, u

    def tile_scan(a, u, reverse):
        for s in (1, 2, 4):
            sh = (R - s) if reverse else s
            a_sh = pltpu.roll(a, sh, axis=0)
            u_sh = pltpu.roll(u, sh, axis=0)
            m = (rowm < 8 - s) if reverse else (rowm >= s)
            u = jnp.where(m, a * u_sh + u, u)
            a = jnp.where(m, a * a_sh, a)
        return a, u

    def step1(c, hprev):
        r0 = pl.multiple_of(c * R, R)
        xc = xc_scr[pl.ds(r0, R), :]
        zz = _dot(xc.astype(BF16), w_ref[...]) + b_ref[...]
        a_f, u_f = gates(zz[:, 0:W], zz[:, W:2 * W], sp_f, xc)
        a_f, u_f = tile_scan(a_f, u_f, False)
        h = hprev
        hs = []
        for t in range(R // 8):
            ht = u_f[t * 8:(t + 1) * 8] + a_f[t * 8:(t + 1) * 8] * h
            hs.append(ht)
            h = ht[7:8]
        hf_scr[pl.ds(r0, R), :] = jnp.concatenate(hs, axis=0)
        a_b, u_b = gates(zz[:, 2 * W:3 * W], zz[:, 3 * W:4 * W], sp_b, xc)
        a_b, u_b = tile_scan(a_b, u_b, True)
        ab_scr[pl.ds(r0, R), :] = a_b
        ub_scr[pl.ds(r0, R), :] = u_b
        return h

    def step2(t, hnext):
        c = n - 1 - t
        r0 = pl.multiple_of(c * R, R)
        a = ab_scr[pl.ds(r0, R), :]
        u = ub_scr[pl.ds(r0, R), :]
        h = hnext
        hs = [None] * (R // 8)
        for tt in reversed(range(R // 8)):
            ht = u[tt * 8:(tt + 1) * 8] + a[tt * 8:(tt + 1) * 8] * h
            hs[tt] = ht
            h = ht[0:1]
        hb = jnp.concatenate(hs, axis=0)
        o_ref[pl.ds(r0, R), :] = (hf_scr[pl.ds(r0, R), :] + hb) * _silu(gt_ref[pl.ds(r0, R), :])
        return h

    if has_ctx:
        h0f = h0_ref[0:1, :]
        h0b = h0_ref[1:2, :]
    else:
        h0f = jnp.zeros((1, W), F32)
        h0b = jnp.zeros((1, W), F32)
    if n <= 2:
        lf = h0f
        for c in range(n):
            lf = step1(c, lf)
        lb = h0b
        for t in range(n):
            lb = step2(t, lb)
    else:
        lf = lax.fori_loop(0, n, step1, h0f)
        lb = lax.fori_loop(0, n, step2, h0b)
    s_ref[0:1, :] = lf
    s_ref[1:2, :] = lb


def _rglru(pm3, conv_w, conv_b, w_gate, b_gate, lam, h0):
    B, L, _ = pm3.shape
    has_ctx = h0 is not None
    W = LRU_BS
    in_specs = [
        pl.BlockSpec((None, L, W), lambda b, n: (b, 0, OFF_LX // W + n)),
        pl.BlockSpec((None, L, W), lambda b, n: (b, 0, OFF_LG // W + n)),
        pl.BlockSpec((4, W), lambda b, n: (0, n)),
        pl.BlockSpec((1, W), lambda b, n: (0, n)),
        pl.BlockSpec((None, W, 4 * W), lambda b, n: (n, 0, 0)),
        pl.BlockSpec((None, 1, 4 * W), lambda b, n: (n, 0, 0)),
        pl.BlockSpec((2, W), lambda b, n: (0, n)),
    ]
    args = [pm3, pm3, conv_w, conv_b, w_gate, b_gate, lam]
    if has_ctx:
        in_specs.append(pl.BlockSpec((None, 2, W), lambda b, n: (b, 0, n)))
        args.append(h0)
    return pl.pallas_call(
        functools.partial(_lru_kernel, L=L, has_ctx=has_ctx),
        grid=(B, LRU_BLOCKS),
        in_specs=in_specs,
        out_specs=[
            pl.BlockSpec((None, L, W), lambda b, n: (b, 0, n)),
            pl.BlockSpec((None, 2, W), lambda b, n: (b, 0, n)),
        ],
        out_shape=[
            jax.ShapeDtypeStruct((B, L, D), F32),
            jax.ShapeDtypeStruct((B, 2, D), F32),
        ],
        scratch_shapes=[pltpu.VMEM((L, W), F32)] * 4,
        compiler_params=_cparams(("arbitrary", "arbitrary")),
    )(*args)


def _att_kernel(*refs, L, has_ctx):
    if has_ctx:
        (sink_ref, q_ref, ag_ref, k_ref, v_ref, ck_ref, cv_ref, cq_ref, sq_ref, cosk_ref, sink_k_ref,
         o_ref, kr_scr, vb_scr) = refs
    else:
        (sink_ref, q_ref, ag_ref, k_ref, v_ref, o_ref, kr_scr, vb_scr) = refs
    Q = CHUNK
    HD = ATT_HD
    G = ATT_GROUP
    qi = pl.program_id(1)

    @pl.when(qi == 0)
    def _():
        kk = k_ref[...]
        if has_ctx:
            ck2 = jnp.concatenate([cosk_ref[...]] * 2, axis=1)
            sk2 = jnp.concatenate([sink_k_ref[...]] * 2, axis=1)
            kk = _rope(kk, ck2, sk2, 16)
        kr_scr[...] = kk.astype(BF16)
        vb_scr[...] = v_ref[...].astype(BF16)

    q = q_ref[...] * (HD ** -0.5)
    if has_ctx:
        q = _rope(q, jnp.concatenate([cq_ref[...]] * 8, axis=1),
                  jnp.concatenate([sq_ref[...]] * 8, axis=1), 16)
        span = Q + 2 * WINDOW
        start = pl.multiple_of(jnp.clip((qi - 1) * Q, 0, L - span), Q)
        kw = kr_scr[pl.ds(start, span), :].astype(F32)
        vw = vb_scr[pl.ds(start, span), :].astype(F32)
        qpos = qi * Q + lax.broadcasted_iota(jnp.int32, (Q, span), 0)
        kpos = start + lax.broadcasted_iota(jnp.int32, (Q, span), 1)
        band = jnp.abs(qpos - kpos) <= WINDOW
        band4 = jnp.concatenate([band] * G, axis=0)
        ckf = ck_ref[...]
        cvf = cv_ref[...]
    else:
        kw = kr_scr[...].astype(F32)
        vw = vb_scr[...].astype(F32)

    outs = []
    for h in range(ATT_KV_HEADS):
        q4 = jnp.concatenate(
            [q[:, (h * G + g) * HD:(h * G + g + 1) * HD] for g in range(G)], axis=0).astype(BF16)
        kh = kw[:, h * HD:(h + 1) * HD].astype(BF16)
        vh = vw[:, h * HD:(h + 1) * HD].astype(BF16)
        sk = jnp.concatenate(
            [jnp.full((Q, 1), sink_ref[h * G + g], F32) for g in range(G)], axis=0)
        s1 = _dot_nt(q4, kh)
        if has_ctx:
            s1 = jnp.where(band4, s1, -jnp.inf)
            ckh = ckf[:, h * HD:(h + 1) * HD].astype(BF16)
            cvh = cvf[:, h * HD:(h + 1) * HD].astype(BF16)
            s2 = _dot_nt(q4, ckh)
            m = jnp.maximum(jnp.maximum(jnp.max(s1, axis=-1, keepdims=True),
                                        jnp.max(s2, axis=-1, keepdims=True)), sk)
            p1 = jnp.exp(s1 - m)
            p2 = jnp.exp(s2 - m)
            den = (jnp.sum(p1, axis=-1, keepdims=True) + jnp.sum(p2, axis=-1, keepdims=True)
                   + jnp.exp(sk - m))
            o = (_dot(p1.astype(BF16), vh) + _dot(p2.astype(BF16), cvh)) / den
        else:
            m = jnp.maximum(jnp.max(s1, axis=-1, keepdims=True), sk)
            p1 = jnp.exp(s1 - m)
            den = jnp.sum(p1, axis=-1, keepdims=True) + jnp.exp(sk - m)
            o = _dot(p1.astype(BF16), vh) / den
        outs += [o[g * Q:(g + 1) * Q] for g in range(G)]
    ao = jnp.concatenate(outs, axis=1)
    o_ref[...] = ao * _silu(ag_ref[...])


def _attention(pm3, sink, ropes, ck, cv):
    B, L, _ = pm3.shape
    has_ctx = ck is not None
    Q = CHUNK
    in_specs = [
        pl.BlockSpec(memory_space=pltpu.SMEM),
        pl.BlockSpec((None, Q, D), lambda b, i: (b, i, OFF_AQ // D)),
        pl.BlockSpec((None, Q, D), lambda b, i: (b, i, OFF_AG // D)),
        pl.BlockSpec((None, L, ATT_KV_W), lambda b, i: (b, 0, OFF_AK // ATT_KV_W)),
        pl.BlockSpec((None, L, ATT_KV_W), lambda b, i: (b, 0, OFF_AV // ATT_KV_W)),
    ]
    args = [sink, pm3, pm3, pm3, pm3]
    if has_ctx:
        P = ck.shape[1]
        in_specs += [
            pl.BlockSpec((None, P, ATT_KV_W), lambda b, i: (b, 0, 0)),
            pl.BlockSpec((None, P, ATT_KV_W), lambda b, i: (b, 0, 0)),
            pl.BlockSpec((Q, 128), lambda b, i: (i, 0)),
            pl.BlockSpec((Q, 128), lambda b, i: (i, 0)),
            pl.BlockSpec((L, 128), lambda b, i: (0, 0)),
            pl.BlockSpec((L, 128), lambda b, i: (0, 0)),
        ]
        args += [ck, cv, ropes[0], ropes[1], ropes[0], ropes[1]]
    return pl.pallas_call(
        functools.partial(_att_kernel, L=L, has_ctx=has_ctx),
        grid=(B, L // Q),
        in_specs=in_specs,
        out_specs=pl.BlockSpec((None, Q, D), lambda b, i: (b, i, 0)),
        out_shape=jax.ShapeDtypeStruct((B, L, D), F32),
        scratch_shapes=[pltpu.VMEM((L, ATT_KV_W), BF16), pltpu.VMEM((L, ATT_KV_W), BF16)],
        compiler_params=_cparams(("arbitrary", "arbitrary")),
    )(*args)


OUT_TM = 256


def _out_kernel(x_ref, ro_ref, lo_ref, ao_ref, g0_ref, g1_ref, g2_ref, mod_ref, gp_ref,
                wb_ref, wo_ref, y_ref):
    merged = (g0_ref[...] * _dot(ro_ref[...].astype(BF16), wb_ref[0])
              + g1_ref[...] * _dot(lo_ref[...].astype(BF16), wb_ref[1])
              + g2_ref[...] * _dot(ao_ref[...].astype(BF16), wb_ref[2]))
    out = _dot(merged.astype(BF16), wo_ref[...])
    nrm = out * lax.rsqrt(jnp.mean(out * out, axis=-1, keepdims=True) + EPS) * gp_ref[...]
    y_ref[...] = x_ref[...] + mod_ref[:, 2 * D:3 * D] * nrm


def _merge_out(x2d, ro, lo, ao, pm2, mod3, row_fn, g_post, w_branch, w_out):
    T = x2d.shape[0]
    tm = OUT_TM
    tok = lambda i: (i, 0)
    return pl.pallas_call(
        _out_kernel,
        grid=(T // tm,),
        in_specs=[
            pl.BlockSpec((tm, D), tok),
            pl.BlockSpec((tm, D), tok),
            pl.BlockSpec((tm, D), tok),
            pl.BlockSpec((tm, D), tok),
            pl.BlockSpec((tm, D), lambda i: (i, 0)),
            pl.BlockSpec((tm, D), lambda i: (i, 1)),
            pl.BlockSpec((tm, D), lambda i: (i, 2)),
            pl.BlockSpec((None, 1, 3 * D), lambda i: (row_fn(i * tm), 0, 0)),
            pl.BlockSpec((1, D), lambda i: (0, 0)),
            pl.BlockSpec((N_BRANCH, D, D), lambda i: (0, 0, 0)),
            pl.BlockSpec((D, D), lambda i: (0, 0)),
        ],
        out_specs=pl.BlockSpec((tm, D), tok),
        out_shape=jax.ShapeDtypeStruct((T, D), F32),
        compiler_params=_cparams(("arbitrary",)),
    )(x2d, ro, lo, ao, pm2, pm2, pm2, mod3, g_post, w_branch, w_out)


def _layer(x, mod3, row_of_token, lw, ctx, ropes_ret, ropes_att):
    B, L, _ = x.shape
    x2d = x.reshape(B * L, D)
    pm2 = _in_proj(x2d, mod3, lambda i: row_of_token(i * IN_TM), lw['g_pre'], lw['w_all'], lw['b_all'])
    pm3 = pm2.reshape(B, L, W_ALL)
    if ctx is None:
        ro, ret_s = _retention(pm3, lw['rd4'], None, None)
        lo, lru_s = _rglru(pm3, lw['conv_w'], lw['conv_b'], lw['w_gate'], lw['b_gate'], lw['lam'], None)
        ao = _attention(pm3, lw['sink'], None, None, None)
    else:
        s_ret, s_lru, ck, cv = ctx
        ro, ret_s = _retention(pm3, lw['rd4'], ropes_ret, s_ret)
        lo, lru_s = _rglru(pm3, lw['conv_w'], lw['conv_b'], lw['w_gate'], lw['b_gate'], lw['lam'], s_lru)
        ao = _attention(pm3, lw['sink'], ropes_att, ck, cv)
    y = _merge_out(x2d, ro.reshape(B * L, D), lo.reshape(B * L, D), ao.reshape(B * L, D), pm2,
                   mod3, row_of_token, lw['g_post'], lw['w_branch'], lw['w_out'])
    return y.reshape(B, L, D), pm3, ret_s, lru_s


def _layer_weights(l, norm_pre, norm_post, w_in, ret_decay, conv_w, conv_b, lru_wa, lru_ba, lru_wx,
                   lru_bx, lru_lambda, att_sink, w_branch, w_merge, b_merge, w_out):
    offs = np.cumsum((512, 512, 1024, 1024, 1024, 1024, 1024, 256, 256, 1024))[:-1].tolist()
    rq, rk, rv, rg, lx, lgt, aq, ak, av, ag = jnp.split(w_in[l], offs, axis=-1)
    w_all = jnp.concatenate([w_merge[l], aq, ag, rv, rg, lx, lgt, rq, rk, ak, av], axis=-1).astype(BF16)
    b_all = jnp.concatenate([b_merge[l], jnp.zeros((W_ALL - N_BRANCH * D,), F32)]).reshape(1, W_ALL)
    w_gate = jnp.concatenate([lru_wa[l, 0], lru_wx[l, 0], lru_wa[l, 1], lru_wx[l, 1]], axis=-1).astype(BF16)
    b4 = jnp.stack([lru_ba[l, 0], lru_bx[l, 0], lru_ba[l, 1], lru_bx[l, 1]], axis=0)
    b_gate = b4.reshape(4, LRU_BLOCKS, LRU_BS).transpose(1, 0, 2).reshape(LRU_BLOCKS, 1, 4 * LRU_BS)
    rd4 = jnp.broadcast_to(ret_decay[l][:, :, None, None], (2, RET_HEADS, 1, RET_DV))
    return {
        'g_pre': norm_pre[l].reshape(1, D), 'g_post': norm_post[l].reshape(1, D),
        'w_all': w_all, 'b_all': b_all, 'rd4': rd4,
        'conv_w': conv_w[l], 'conv_b': conv_b[l].reshape(1, D),
        'w_gate': w_gate, 'b_gate': b_gate, 'lam': lru_lambda[l],
        'sink': att_sink[l], 'w_branch': w_branch[l].astype(BF16), 'w_out': w_out[l].astype(BF16),
    }


def kernel(x_prompt, x_sample, cache_k, cache_v, state_ret, state_lru, c, c_ctx, w_ada, b_ada, norm_pre, norm_post, w_in, ret_decay, conv_w, conv_b, lru_wa, lru_ba, lru_wx, lru_bx, lru_lambda, att_sink, w_branch, w_merge, b_merge, w_out):
    B, S, _ = x_prompt.shape
    Bd, Ld, _ = x_sample.shape
    P = cache_k.shape[2]
    cond8 = jnp.concatenate([c_ctx[None, :], c, jnp.zeros((8 - 1 - Bd, D), F32)], axis=0)
    mod = _ada_mod(cond8, w_ada, b_ada)
    ropes_ret = _rope_tables(Ld, RET_DK)
    ropes_att = _rope_tables(Ld, ATT_HD)

    yp, ys = x_prompt, x_sample
    new_k, new_v, new_ret, new_lru = [], [], [], []
    for l in range(DEPTH):
        lw = _layer_weights(l, norm_pre, norm_post, w_in, ret_decay, conv_w, conv_b, lru_wa, lru_ba,
                            lru_wx, lru_bx, lru_lambda, att_sink, w_branch, w_merge, b_merge, w_out)
        mod3 = mod[l].reshape(8, 1, 3 * D)
        yp, pm_p, r_s, l_s = _layer(yp, mod3, lambda t: 0, lw, None, None, None)
        new_ret.append(r_s)
        new_lru.append(l_s)
        new_k.append(pm_p[:, :, OFF_AK:OFF_AK + ATT_KV_W].reshape(B, S, ATT_KV_HEADS, ATT_HD))
        new_v.append(pm_p[:, :, OFF_AV:OFF_AV + ATT_KV_W].reshape(B, S, ATT_KV_HEADS, ATT_HD))
        ctx = (state_ret[:, l], state_lru[:, l],
               cache_k[:, l].reshape(Bd, P, ATT_KV_W), cache_v[:, l].reshape(Bd, P, ATT_KV_W))
        ys, _, _, _ = _layer(ys, mod3, lambda t: 1 + t // Ld, lw, ctx, ropes_ret, ropes_att)
    return (yp, ys, jnp.stack(new_k, axis=1), jnp.stack(new_v, axis=1),
            jnp.stack(new_ret, axis=1), jnp.stack(new_lru, axis=1))
```

```python
import functools

import numpy as np
import jax
import jax.numpy as jnp
from jax import lax
from jax.experimental import pallas as pl
from jax.experimental.pallas import tpu as pltpu

F32 = jnp.float32
BF16 = jnp.bfloat16

D = 1024
DEPTH = 2
GRID_W = 64
EPS = 1e-6
N_BRANCH = 3
RET_HEADS = 4
RET_DV = 256
RET_DK = 128
CHUNK = 128
LRU_BLOCKS = 8
LRU_BS = 128
LRU_C = 8.0
LRU_GROUP = 64
LRU_ITER_GROUPS = 4
ATT_HD = 64
ATT_Q_HEADS = 16
ATT_KV_HEADS = 4
ATT_GROUP = 4
ATT_KV_W = 256
WINDOW = 128
ROPE_BASE = 10000.0

OFF_AQ = 0
OFF_AG = 1024
OFF_RV = 2048
OFF_RG = 3072
OFF_LX = 4096
OFF_LG = 5120
OFF_RQ = 6144
OFF_RK = 6656
OFF_AK = 7168
OFF_AV = 7424
W_IN = 7680

VMEM_LIMIT = 56 * 1024 * 1024


def _cparams(sem):
    return pltpu.CompilerParams(dimension_semantics=sem, vmem_limit_bytes=VMEM_LIMIT)


def _sigmoid(x):
    return 0.5 * jnp.tanh(0.5 * x) + 0.5


def _silu(x):
    return x * _sigmoid(x)


def _softplus(z):
    return jnp.maximum(z, 0.0) + jnp.log1p(jnp.exp(-jnp.abs(z)))


def _dot(a, b):
    return jnp.dot(a, b, preferred_element_type=F32)


def _dot_nt(a, b):
    return lax.dot_general(a, b, (((1,), (1,)), ((), ())), preferred_element_type=F32)


def _dot_tn(a, b):
    return lax.dot_general(a, b, (((0,), (0,)), ((), ())), preferred_element_type=F32)


def _modulated_norm(x, g, mod):
    y = x * lax.rsqrt(jnp.mean(x * x, axis=-1, keepdims=True) + EPS) * g
    return y * (1.0 + mod[:, D:2 * D]) + mod[:, 0:D]


def _ada_kernel(c_ref, w_ref, b_ref, o_ref):
    s = _silu(c_ref[...])
    o_ref[...] = _dot(s.astype(BF16), w_ref[...].astype(BF16)) + b_ref[...]


def _ada_mod(cond8, w_ada, b_ada):
    tn = 1024
    return pl.pallas_call(
        _ada_kernel,
        grid=(DEPTH, 3 * D // tn),
        in_specs=[
            pl.BlockSpec((8, D), lambda l, j: (0, 0)),
            pl.BlockSpec((None, D, tn), lambda l, j: (l, 0, j)),
            pl.BlockSpec((None, 1, tn), lambda l, j: (l, 0, j)),
        ],
        out_specs=pl.BlockSpec((None, 8, tn), lambda l, j: (l, 0, j)),
        out_shape=jax.ShapeDtypeStruct((DEPTH, 8, 3 * D), F32),
        compiler_params=_cparams(("arbitrary", "arbitrary")),
    )(cond8, w_ada, b_ada.reshape(DEPTH, 1, 3 * D))


IN_TM = 1024
IN_TN = 1280


def _in_kernel(x_ref, mod_ref, g_ref, w_ref, o_ref, h_all):
    j = pl.program_id(0)
    i = pl.program_id(1)
    r0 = pl.multiple_of(i * IN_TM, IN_TM)

    @pl.when(j == 0)
    def _():
        h = _modulated_norm(x_ref[...], g_ref[...], mod_ref[...])
        h_all[pl.ds(r0, IN_TM), :] = h.astype(BF16)

    o_ref[...] = _dot(h_all[pl.ds(r0, IN_TM), :], w_ref[...])


def _in_proj(x2d, mod3, row_fn, g_pre, w_in):
    T = x2d.shape[0]
    n_m = T // IN_TM
    return pl.pallas_call(
        _in_kernel,
        grid=(W_IN // IN_TN, n_m),
        in_specs=[
            pl.BlockSpec((IN_TM, D), lambda j, i: (jnp.where(j == 0, i, n_m - 1), 0)),
            pl.BlockSpec((None, 1, 3 * D), lambda j, i: (row_fn(i * IN_TM), 0, 0)),
            pl.BlockSpec((1, D), lambda j, i: (0, 0)),
            pl.BlockSpec((D, IN_TN), lambda j, i: (0, j)),
        ],
        out_specs=pl.BlockSpec((IN_TM, IN_TN), lambda j, i: (i, j)),
        out_shape=jax.ShapeDtypeStruct((T, W_IN), F32),
        scratch_shapes=[pltpu.VMEM((T, D), BF16)],
        compiler_params=_cparams(("arbitrary", "arbitrary")),
    )(x2d, mod3, g_pre, w_in)


def _rope(x, cos_t, sin_t, half):
    lane = lax.broadcasted_iota(jnp.int32, x.shape, 1)
    up = pltpu.roll(x, x.shape[1] - half, axis=1)
    dn = pltpu.roll(x, half, axis=1)
    partner = jnp.where((lane & half) == 0, up, dn)
    return x * cos_t + partner * sin_t


def _rope_tables(n_tokens, dim):
    nq = dim // 4
    lane = np.arange(128)
    within = lane % dim
    axis = within // (2 * nq)
    freq = within % nq
    sign = np.where((within % (2 * nq)) < nq, -1.0, 1.0).astype(np.float32)
    inv = ROPE_BASE ** (-jnp.arange(nq, dtype=F32) / nq)
    t = jnp.arange(n_tokens)
    pos = jnp.stack([(t // GRID_W).astype(F32), (t % GRID_W).astype(F32)], axis=1)
    ang = pos[:, axis] * inv[freq][None, :]
    return jnp.cos(ang), jnp.sin(ang) * sign[None, :]


def _ret_kernel(*refs, n_chunks, has_ctx):
    if has_ctx:
        (q_ref, k_ref, v_ref, g_ref, rd_ref, cos_ref, sin_ref, s0_ref,
         o_ref, s_ref, qs, ks, s_all, sf_run, sb_run) = refs
    else:
        (q_ref, k_ref, v_ref, g_ref, rd_ref,
         o_ref, s_ref, qs, ks, s_all, sf_run, sb_run) = refs
    C = CHUNK

    q = q_ref[...] * (RET_DK ** -0.5)
    k = k_ref[...]
    if has_ctx:
        q = _rope(q, cos_ref[...], sin_ref[...], 32)
        k = _rope(k, cos_ref[...], sin_ref[...], 32)
        sf_run[...] = s0_ref[0]
        sb_run[...] = s0_ref[1]
    else:
        sf_run[...] = jnp.zeros_like(sf_run)
        sb_run[...] = jnp.zeros_like(sb_run)
    qs[...] = q
    ks[...] = k

    lg_f = -_softplus(-rd_ref[0])
    lg_b = -_softplus(-rd_ref[1])
    lgf = lg_f[:, :C]
    lgb = lg_b[:, :C]
    ii = lax.broadcasted_iota(jnp.int32, (C, C), 0)
    jj = lax.broadcasted_iota(jnp.int32, (C, C), 1)
    diff = (ii - jj).astype(F32)
    dmat = (jnp.where(diff >= 0, jnp.exp(lgf * jnp.maximum(diff, 0.0)), 0.0)
            + jnp.where(diff <= 0, jnp.exp(lgb * jnp.maximum(-diff, 0.0)), 0.0))
    ri = ii.astype(F32)
    dq_f = jnp.exp(lgf * (ri + 1.0))
    dk_f = jnp.exp(lgf * (C - 1.0 - ri))
    dq_b = jnp.exp(lgb * (C - ri))
    dk_b = jnp.exp(lgb * ri)
    dc_f = jnp.exp(lg_f * float(C))
    dc_b = jnp.exp(lg_b * float(C))

    def state_step(t, carry):
        cf = t
        cb = n_chunks - 1 - t
        rf = pl.multiple_of(cf * C, C)
        rb = pl.multiple_of(cb * C, C)
        sf = sf_run[...]
        sb = sb_run[...]
        s_all[cf, 0:RET_DK, :] = sf.astype(BF16)
        s_all[cb, RET_DK:2 * RET_DK, :] = sb.astype(BF16)
        kf = (ks[pl.ds(rf, C), :] * dk_f).astype(BF16)
        kb = (ks[pl.ds(rb, C), :] * dk_b).astype(BF16)
        sf_run[...] = sf * dc_f + _dot_tn(kf, v_ref[pl.ds(rf, C), :].astype(BF16))
        sb_run[...] = sb * dc_b + _dot_tn(kb, v_ref[pl.ds(rb, C), :].astype(BF16))
        return carry

    def out_step(c, carry):
        r0 = pl.multiple_of(c * C, C)
        qc = qs[pl.ds(r0, C), :]
        kc = ks[pl.ds(r0, C), :]
        vc = v_ref[pl.ds(r0, C), :].astype(BF16)
        a = _dot_nt(qc.astype(BF16), kc.astype(BF16)) * dmat
        qq = jnp.concatenate([qc * dq_f, qc * dq_b], axis=1).astype(BF16)
        o = _dot(a.astype(BF16), vc) + _dot(qq, s_all[c])
        mu = jnp.mean(o, axis=-1, keepdims=True)
        oc = o - mu
        on = oc * lax.rsqrt(jnp.mean(oc * oc, axis=-1, keepdims=True) + EPS)
        o_ref[pl.ds(r0, C), :] = on * _silu(g_ref[pl.ds(r0, C), :])
        return carry

    if n_chunks <= 2:
        for t in range(n_chunks):
            state_step(t, 0)
        for c in range(n_chunks):
            out_step(c, 0)
    else:
        lax.fori_loop(0, n_chunks, state_step, 0)
        lax.fori_loop(0, n_chunks, out_step, 0, unroll=2)
    s_ref[0] = sf_run[...]
    s_ref[1] = sb_run[...]


def _retention(pm3, rd4, ropes, s0):
    B, L, _ = pm3.shape
    n_chunks = L // CHUNK
    has_ctx = s0 is not None
    H = RET_HEADS
    in_specs = [
        pl.BlockSpec((None, L, RET_DK), lambda b, h: (b, 0, OFF_RQ // RET_DK + h)),
        pl.BlockSpec((None, L, RET_DK), lambda b, h: (b, 0, OFF_RK // RET_DK + h)),
        pl.BlockSpec((None, L, RET_DV), lambda b, h: (b, 0, OFF_RV // RET_DV + h)),
        pl.BlockSpec((None, L, RET_DV), lambda b, h: (b, 0, OFF_RG // RET_DV + h)),
        pl.BlockSpec((2, None, 1, RET_DV), lambda b, h: (0, h, 0, 0)),
    ]
    args = [pm3, pm3, pm3, pm3, rd4]
    if has_ctx:
        in_specs += [
            pl.BlockSpec((L, 128), lambda b, h: (0, 0)),
            pl.BlockSpec((L, 128), lambda b, h: (0, 0)),
            pl.BlockSpec((None, 2, None, RET_DK, RET_DV), lambda b, h: (b, 0, h, 0, 0)),
        ]
        args += [ropes[0], ropes[1], s0]
    return pl.pallas_call(
        functools.partial(_ret_kernel, n_chunks=n_chunks, has_ctx=has_ctx),
        grid=(B, H),
        in_specs=in_specs,
        out_specs=[
            pl.BlockSpec((None, L, RET_DV), lambda b, h: (b, 0, h)),
            pl.BlockSpec((None, 2, None, RET_DK, RET_DV), lambda b, h: (b, 0, h, 0, 0)),
        ],
        out_shape=[
            jax.ShapeDtypeStruct((B, L, D), F32),
            jax.ShapeDtypeStruct((B, 2, H, RET_DK, RET_DV), F32),
        ],
        scratch_shapes=[
            pltpu.VMEM((L, RET_DK), F32),
            pltpu.VMEM((L, RET_DK), F32),
            pltpu.VMEM((n_chunks, 2 * RET_DK, RET_DV), BF16),
            pltpu.VMEM((RET_DK, RET_DV), F32),
            pltpu.VMEM((RET_DK, RET_DV), F32),
        ],
        compiler_params=_cparams(("arbitrary", "arbitrary")),
    )(*args)


def _lru_kernel(*refs, L, has_ctx):
    if has_ctx:
        (x_ref, gt_ref, cw_ref, cb_ref, w_ref, b_ref, lam_ref, h0_ref,
         o_ref, s_ref, xc_scr, hf_scr, pb_scr, ub_scr) = refs
    else:
        (x_ref, gt_ref, cw_ref, cb_ref, w_ref, b_ref, lam_ref,
         o_ref, s_ref, xc_scr, hf_scr, pb_scr, ub_scr) = refs
    G = LRU_GROUP
    NG = LRU_ITER_GROUPS
    n_it = L // (NG * G)
    W = LRU_BS

    x = x_ref[...]
    row = lax.broadcasted_iota(jnp.int32, (L, W), 0)
    xm1 = jnp.where(row >= 1, pltpu.roll(x, 1, axis=0), 0.0)
    xp1 = jnp.where(row < L - 1, pltpu.roll(x, L - 1, axis=0), 0.0)
    xp2 = jnp.where(row < L - 2, pltpu.roll(x, L - 2, axis=0), 0.0)
    xc_scr[...] = (cw_ref[0:1, :] * xm1 + cw_ref[1:2, :] * x + cw_ref[2:3, :] * xp1
                   + cw_ref[3:4, :] * xp2 + cb_ref[...])

    nsp_f = -LRU_C * _softplus(-lam_ref[0:1, :])
    nsp_b = -LRU_C * _softplus(-lam_ref[1:2, :])
    row8 = lax.broadcasted_iota(jnp.int32, (8, W), 0)

    def gates(zr, zi, nsp, xc):
        log_a = _sigmoid(zr) * nsp
        a = jnp.exp(log_a)
        u = jnp.sqrt(-jnp.tanh(log_a) * (a * a + 1.0)) * (_sigmoid(zi) * xc)
        return a, u

    def seg_scan(p, hl, carry, reverse):
        a, u = p, hl
        for s in (1, 2, 4):
            sh = (8 - s) if reverse else s
            a_sh = pltpu.roll(a, sh, axis=0)
            u_sh = pltpu.roll(u, sh, axis=0)
            m = (row8 < 8 - s) if reverse else (row8 >= s)
            u = jnp.where(m, a * u_sh + u, u)
            a = jnp.where(m, a * a_sh, a)
        h_end = u + a * carry
        if reverse:
            h_in = jnp.where(row8 == 7, carry, pltpu.roll(h_end, 7, axis=0))
            return h_in, h_end[0:1]
        h_in = jnp.where(row8 == 0, carry, pltpu.roll(h_end, 1, axis=0))
        return h_in, h_end[7:8]

    def pass1(it, carry):
        base = pl.multiple_of(it * (NG * G), NG * G)
        xg = jnp.concatenate([xc_scr[pl.ds(base + gi * G + r, 8, stride=8), :]
                              for gi in range(NG) for r in range(8)], axis=0)
        zz = _dot(xg.astype(BF16), w_ref[...]) + b_ref[...]
        a_f, u_f = gates(zz[:, 0:W], zz[:, W:2 * W], nsp_f, xg)
        a_b, u_b = gates(zz[:, 2 * W:3 * W], zz[:, 3 * W:4 * W], nsp_b, xg)
        for gi in range(NG):
            v = lambda arr, r: arr[gi * G + r * 8:gi * G + (r + 1) * 8]
            p = [v(a_f, 0)]
            hl = [v(u_f, 0)]
            for r in range(1, 8):
                hl.append(v(a_f, r) * hl[-1] + v(u_f, r))
                p.append(v(a_f, r) * p[-1])
            h_in, carry = seg_scan(p[7], hl[7], carry, False)
            hf_scr[pl.ds(base + gi * G, G), :] = jnp.concatenate(
                [hl[r] + p[r] * h_in for r in range(8)], axis=0)
            pb = [None] * 8
            hb = [None] * 8
            pb[7] = v(a_b, 7)
            hb[7] = v(u_b, 7)
            for r in range(6, -1, -1):
                hb[r] = v(a_b, r) * hb[r + 1] + v(u_b, r)
                pb[r] = v(a_b, r) * pb[r + 1]
            pb_scr[pl.ds(base + gi * G, G), :] = jnp.concatenate(pb, axis=0)
            ub_scr[pl.ds(base + gi * G, G), :] = jnp.concatenate(hb, axis=0)
        return carry

    def pass2(t, carry):
        base = pl.multiple_of((n_it - 1 - t) * (NG * G), NG * G)
        for gi in range(NG - 1, -1, -1):
            gb = base + gi * G
            pb = pb_scr[pl.ds(gb, G), :]
            hb = ub_scr[pl.ds(gb, G), :]
            hf = hf_scr[pl.ds(gb, G), :]
            h_in, carry = seg_scan(pb[0:8], hb[0:8], carry, True)
            for r in range(8):
                sl = slice(r * 8, (r + 1) * 8)
                gt = gt_ref[pl.ds(gb + r, 8, stride=8), :]
                o_ref[pl.ds(gb + r, 8, stride=8), :] = (hf[sl] + hb[sl] + pb[sl] * h_in) * _silu(gt)
        return carry

    if has_ctx:
        h0f = h0_ref[0:1, :]
        h0b = h0_ref[1:2, :]
    else:
        h0f = jnp.zeros((1, W), F32)
        h0b = jnp.zeros((1, W), F32)
    if n_it == 1:
        s_ref[0:1, :] = pass1(0, h0f)
        s_ref[1:2, :] = pass2(0, h0b)
    else:
        s_ref[0:1, :] = lax.fori_loop(0, n_it, pass1, h0f)
        s_ref[1:2, :] = lax.fori_loop(0, n_it, pass2, h0b)


def _rglru(pm3, conv_w, conv_b, w_gate, b_gate, lam, h0):
    B, L, _ = pm3.shape
    has_ctx = h0 is not None
    W = LRU_BS
    in_specs = [
        pl.BlockSpec((None, L, W), lambda b, n: (b, 0, OFF_LX // W + n)),
        pl.BlockSpec((None, L, W), lambda b, n: (b, 0, OFF_LG // W + n)),
        pl.BlockSpec((4, W), lambda b, n: (0, n)),
        pl.BlockSpec((1, W), lambda b, n: (0, n)),
        pl.BlockSpec((None, W, 4 * W), lambda b, n: (n, 0, 0)),
        pl.BlockSpec((None, 1, 4 * W), lambda b, n: (n, 0, 0)),
        pl.BlockSpec((2, W), lambda b, n: (0, n)),
    ]
    args = [pm3, pm3, conv_w, conv_b, w_gate, b_gate, lam]
    if has_ctx:
        in_specs.append(pl.BlockSpec((None, 2, W), lambda b, n: (b, 0, n)))
        args.append(h0)
    return pl.pallas_call(
        functools.partial(_lru_kernel, L=L, has_ctx=has_ctx),
        grid=(B, LRU_BLOCKS),
        in_specs=in_specs,
        out_specs=[
            pl.BlockSpec((None, L, W), lambda b, n: (b, 0, n)),
            pl.BlockSpec((None, 2, W), lambda b, n: (b, 0, n)),
        ],
        out_shape=[
            jax.ShapeDtypeStruct((B, L, D), F32),
            jax.ShapeDtypeStruct((B, 2, D), F32),
        ],
        scratch_shapes=[pltpu.VMEM((L, W), F32)] * 4,
        compiler_params=_cparams(("arbitrary", "arbitrary")),
    )(*args)


def _att_kernel(*refs, L, has_ctx):
    if has_ctx:
        (sink_ref, q_ref, ag_ref, k_ref, v_ref, ck_ref, cv_ref, cosq_ref, sinq_ref, cosk_ref, sink_tab_ref,
         o_ref, km, vm, ckm, cvm) = refs
    else:
        (sink_ref, q_ref, ag_ref, k_ref, v_ref, o_ref, km, vm) = refs
    Q = CHUNK
    G = ATT_GROUP
    KW = ATT_KV_W
    qi = pl.program_id(1)

    def head_masked(dst, val):
        head = lax.shift_right_logical(lax.broadcasted_iota(jnp.int32, val.shape, 1), 6)
        for h in range(ATT_KV_HEADS):
            dst[h] = jnp.where(head == h, val, 0.0).astype(BF16)

    @pl.when(qi == 0)
    def _():
        kk = k_ref[...]
        if has_ctx:
            kk = _rope(kk, jnp.concatenate([cosk_ref[...]] * 2, axis=1),
                       jnp.concatenate([sink_tab_ref[...]] * 2, axis=1), 16)
            head_masked(ckm, ck_ref[...])
            head_masked(cvm, cv_ref[...])
        head_masked(km, kk)
        head_masked(vm, v_ref[...])

    q = q_ref[...] * (ATT_HD ** -0.5)
    if has_ctx:
        q = _rope(q, jnp.concatenate([cosq_ref[...]] * 8, axis=1),
                  jnp.concatenate([sinq_ref[...]] * 8, axis=1), 16)
        span = Q + 2 * WINDOW
        start = pl.multiple_of(jnp.clip((qi - 1) * Q, 0, L - span), Q)
        qpos = qi * Q + lax.broadcasted_iota(jnp.int32, (Q, span), 0)
        kpos = start + lax.broadcasted_iota(jnp.int32, (Q, span), 1)
        band = jnp.abs(qpos - kpos) <= WINDOW
        band4 = jnp.concatenate([band] * G, axis=0)
    qg = jnp.concatenate([q[:, g * KW:(g + 1) * KW] for g in range(G)], axis=0).astype(BF16)

    o = None
    for h in range(ATT_KV_HEADS):
        sk = jnp.concatenate(
            [jnp.full((Q, 1), sink_ref[h * G + g], F32) for g in range(G)], axis=0)
        if has_ctx:
            s1 = jnp.where(band4, _dot_nt(qg, km[h, pl.ds(start, span), :]), -jnp.inf)
            s2 = _dot_nt(qg, ckm[h])
            m = jnp.maximum(jnp.maximum(jnp.max(s1, axis=-1, keepdims=True),
                                        jnp.max(s2, axis=-1, keepdims=True)), sk)
            p1 = jnp.exp(s1 - m)
            p2 = jnp.exp(s2 - m)
            den = (jnp.sum(p1, axis=-1, keepdims=True) + jnp.sum(p2, axis=-1, keepdims=True)
                   + jnp.exp(sk - m))
            oh = _dot(p1.astype(BF16), vm[h, pl.ds(start, span), :]) + _dot(p2.astype(BF16), cvm[h])
        else:
            s1 = _dot_nt(qg, km[h])
            m = jnp.maximum(jnp.max(s1, axis=-1, keepdims=True), sk)
            p1 = jnp.exp(s1 - m)
            den = jnp.sum(p1, axis=-1, keepdims=True) + jnp.exp(sk - m)
            oh = _dot(p1.astype(BF16), vm[h])
        oh = oh * (1.0 / den)
        o = oh if o is None else o + oh
    ao = jnp.concatenate([o[g * Q:(g + 1) * Q] for g in range(G)], axis=1)
    o_ref[...] = ao * _silu(ag_ref[...])


def _attention(pm3, sink, ropes, ck, cv):
    B, L, _ = pm3.shape
    has_ctx = ck is not None
    Q = CHUNK
    KW = ATT_KV_W
    in_specs = [
        pl.BlockSpec(memory_space=pltpu.SMEM),
        pl.BlockSpec((None, Q, D), lambda b, i: (b, i, OFF_AQ // D)),
        pl.BlockSpec((None, Q, D), lambda b, i: (b, i, OFF_AG // D)),
        pl.BlockSpec((None, L, KW), lambda b, i: (b, 0, OFF_AK // KW)),
        pl.BlockSpec((None, L, KW), lambda b, i: (b, 0, OFF_AV // KW)),
    ]
    args = [sink, pm3, pm3, pm3, pm3]
    scratch = [pltpu.VMEM((ATT_KV_HEADS, L, KW), BF16), pltpu.VMEM((ATT_KV_HEADS, L, KW), BF16)]
    if has_ctx:
        P = ck.shape[1]
        in_specs += [
            pl.BlockSpec((None, P, KW), lambda b, i: (b, 0, 0)),
            pl.BlockSpec((None, P, KW), lambda b, i: (b, 0, 0)),
            pl.BlockSpec((Q, 128), lambda b, i: (i, 0)),
            pl.BlockSpec((Q, 128), lambda b, i: (i, 0)),
            pl.BlockSpec((L, 128), lambda b, i: (0, 0)),
            pl.BlockSpec((L, 128), lambda b, i: (0, 0)),
        ]
        args += [ck, cv, ropes[0], ropes[1], ropes[0], ropes[1]]
        scratch += [pltpu.VMEM((ATT_KV_HEADS, P, KW), BF16), pltpu.VMEM((ATT_KV_HEADS, P, KW), BF16)]
    return pl.pallas_call(
        functools.partial(_att_kernel, L=L, has_ctx=has_ctx),
        grid=(B, L // Q),
        in_specs=in_specs,
        out_specs=pl.BlockSpec((None, Q, D), lambda b, i: (b, i, 0)),
        out_shape=jax.ShapeDtypeStruct((B, L, D), F32),
        scratch_shapes=scratch,
        compiler_params=_cparams(("arbitrary", "arbitrary")),
    )(*args)


OUT_TM = 512


def _out_kernel(x_ref, ro_ref, lo_ref, ao_ref, mod_ref, gpre_ref, gpost_ref,
                wm_ref, bm_ref, wb_ref, wo_ref, y_ref):
    x = x_ref[...]
    mod = mod_ref[...]
    h = _modulated_norm(x, gpre_ref[...], mod).astype(BF16)
    merged = None
    for n, br_ref in enumerate((ro_ref, lo_ref, ao_ref)):
        gate = _sigmoid(_dot(h, wm_ref[:, n * D:(n + 1) * D]) + bm_ref[:, n * D:(n + 1) * D])
        z = gate * _dot(br_ref[...].astype(BF16), wb_ref[n])
        merged = z if merged is None else merged + z
    out = _dot(merged.astype(BF16), wo_ref[...])
    nrm = out * lax.rsqrt(jnp.mean(out * out, axis=-1, keepdims=True) + EPS) * gpost_ref[...]
    y_ref[...] = x + mod[:, 2 * D:3 * D] * nrm


def _merge_out(x2d, ro, lo, ao, mod3, row_fn, g_pre, g_post, w_merge, b_merge, w_branch, w_out):
    T = x2d.shape[0]
    tm = OUT_TM
    tok = lambda i: (i, 0)
    once = pl.Buffered(1)
    return pl.pallas_call(
        _out_kernel,
        grid=(T // tm,),
        in_specs=[
            pl.BlockSpec((tm, D), tok),
            pl.BlockSpec((tm, D), tok),
            pl.BlockSpec((tm, D), tok),
            pl.BlockSpec((tm, D), tok),
            pl.BlockSpec((None, 1, 3 * D), lambda i: (row_fn(i * tm), 0, 0)),
            pl.BlockSpec((1, D), lambda i: (0, 0)),
            pl.BlockSpec((1, D), lambda i: (0, 0)),
            pl.BlockSpec((D, N_BRANCH * D), lambda i: (0, 0), pipeline_mode=once),
            pl.BlockSpec((1, N_BRANCH * D), lambda i: (0, 0)),
            pl.BlockSpec((N_BRANCH, D, D), lambda i: (0, 0, 0), pipeline_mode=once),
            pl.BlockSpec((D, D), lambda i: (0, 0), pipeline_mode=once),
        ],
        out_specs=pl.BlockSpec((tm, D), tok),
        out_shape=jax.ShapeDtypeStruct((T, D), F32),
        compiler_params=_cparams(("arbitrary",)),
    )(x2d, ro, lo, ao, mod3, g_pre, g_post, w_merge, b_merge, w_branch, w_out)


def _layer(x, mod3, row_of_token, lw, ctx, ropes_ret, ropes_att):
    B, L, _ = x.shape
    x2d = x.reshape(B * L, D)
    pm2 = _in_proj(x2d, mod3, row_of_token, lw['g_pre'], lw['w_in'])
    pm3 = pm2.reshape(B, L, W_IN)
    if ctx is None:
        ro, ret_s = _retention(pm3, lw['rd4'], None, None)
        lo, lru_s = _rglru(pm3, lw['conv_w'], lw['conv_b'], lw['w_gate'], lw['b_gate'], lw['lam'], None)
        ao = _attention(pm3, lw['sink'], None, None, None)
    else:
        s_ret, s_lru, ck, cv = ctx
        ro, ret_s = _retention(pm3, lw['rd4'], ropes_ret, s_ret)
        lo, lru_s = _rglru(pm3, lw['conv_w'], lw['conv_b'], lw['w_gate'], lw['b_gate'], lw['lam'], s_lru)
        ao = _attention(pm3, lw['sink'], ropes_att, ck, cv)
    y = _merge_out(x2d, ro.reshape(B * L, D), lo.reshape(B * L, D), ao.reshape(B * L, D),
                   mod3, row_of_token, lw['g_pre'], lw['g_post'], lw['w_merge'], lw['b_merge'],
                   lw['w_branch'], lw['w_out'])
    return y.reshape(B, L, D), pm3, ret_s, lru_s


def _group_major(w, axis):
    shp = w.shape
    w = w.reshape(shp[:axis] + (ATT_KV_HEADS, ATT_GROUP, ATT_HD) + shp[axis + 1:])
    w = jnp.swapaxes(w, axis, axis + 1)
    return w.reshape(shp)


def _layer_weights(l, norm_pre, norm_post, w_in, ret_decay, conv_w, conv_b, lru_wa, lru_ba, lru_wx,
                   lru_bx, lru_lambda, att_sink, w_branch, w_merge, b_merge, w_out):
    offs = np.cumsum((512, 512, 1024, 1024, 1024, 1024, 1024, 256, 256, 1024))[:-1].tolist()
    rq, rk, rv, rg, lx, lgt, aq, ak, av, ag = jnp.split(w_in[l], offs, axis=-1)
    w_cat = jnp.concatenate([_group_major(aq, 1), _group_major(ag, 1), rv, rg, lx, lgt, rq, rk, ak, av],
                            axis=-1).astype(BF16)
    wb = jnp.stack([w_branch[l, 0], w_branch[l, 1], _group_major(w_branch[l, 2], 0)], axis=0).astype(BF16)
    w_gate = jnp.concatenate([lru_wa[l, 0], lru_wx[l, 0], lru_wa[l, 1], lru_wx[l, 1]], axis=-1).astype(BF16)
    b4 = jnp.stack([lru_ba[l, 0], lru_bx[l, 0], lru_ba[l, 1], lru_bx[l, 1]], axis=0)
    b_gate = b4.reshape(4, LRU_BLOCKS, LRU_BS).transpose(1, 0, 2).reshape(LRU_BLOCKS, 1, 4 * LRU_BS)
    rd4 = jnp.broadcast_to(ret_decay[l][:, :, None, None], (2, RET_HEADS, 1, RET_DV))
    return {
        'g_pre': norm_pre[l].reshape(1, D), 'g_post': norm_post[l].reshape(1, D),
        'w_in': w_cat, 'rd4': rd4,
        'conv_w': conv_w[l], 'conv_b': conv_b[l].reshape(1, D),
        'w_gate': w_gate, 'b_gate': b_gate, 'lam': lru_lambda[l],
        'sink': att_sink[l], 'w_merge': w_merge[l].astype(BF16), 'b_merge': b_merge[l].reshape(1, N_BRANCH * D),
        'w_branch': wb, 'w_out': w_out[l].astype(BF16),
    }


def kernel(x_prompt, x_sample, cache_k, cache_v, state_ret, state_lru, c, c_ctx, w_ada, b_ada, norm_pre, norm_post, w_in, ret_decay, conv_w, conv_b, lru_wa, lru_ba, lru_wx, lru_bx, lru_lambda, att_sink, w_branch, w_merge, b_merge, w_out):
    B, S, _ = x_prompt.shape
    Bd, Ld, _ = x_sample.shape
    P = cache_k.shape[2]
    cond8 = jnp.concatenate([c_ctx[None, :], c, jnp.zeros((8 - 1 - Bd, D), F32)], axis=0)
    mod = _ada_mod(cond8, w_ada, b_ada)
    ropes_ret = _rope_tables(Ld, RET_DK)
    ropes_att = _rope_tables(Ld, ATT_HD)

    yp, ys = x_prompt, x_sample
    new_k, new_v, new_ret, new_lru = [], [], [], []
    for l in range(DEPTH):
        lw = _layer_weights(l, norm_pre, norm_post, w_in, ret_decay, conv_w, conv_b, lru_wa, lru_ba,
                            lru_wx, lru_bx, lru_lambda, att_sink, w_branch, w_merge, b_merge, w_out)
        mod3 = mod[l].reshape(8, 1, 3 * D)
        yp, pm_p, r_s, l_s = _layer(yp, mod3, lambda t: 0, lw, None, None, None)
        new_ret.append(r_s)
        new_lru.append(l_s)
        new_k.append(pm_p[:, :, OFF_AK:OFF_AK + ATT_KV_W].reshape(B, S, ATT_KV_HEADS, ATT_HD))
        new_v.append(pm_p[:, :, OFF_AV:OFF_AV + ATT_KV_W].reshape(B, S, ATT_KV_HEADS, ATT_HD))
        ctx = (state_ret[:, l], state_lru[:, l],
               cache_k[:, l].reshape(Bd, P, ATT_KV_W), cache_v[:, l].reshape(Bd, P, ATT_KV_W))
        ys, _, _, _ = _layer(ys, mod3, lambda t: 1 + t // Ld, lw, ctx, ropes_ret, ropes_att)
    return (yp, ys, jnp.stack(new_k, axis=1), jnp.stack(new_v, axis=1),
            jnp.stack(new_ret, axis=1), jnp.stack(new_lru, axis=1))
```

```python
import functools

import numpy as np
import jax
import jax.numpy as jnp
from jax import lax
from jax.experimental import pallas as pl
from jax.experimental.pallas import tpu as pltpu

F32 = jnp.float32
BF16 = jnp.bfloat16

D = 1024
DEPTH = 2
GRID_W = 64
EPS = 1e-6
N_BRANCH = 3
RET_HEADS = 4
RET_DV = 256
RET_DK = 128
CHUNK = 128
LRU_BLOCKS = 8
LRU_BS = 128
LRU_C = 8.0
LRU_GROUP = 64
LRU_ITER_GROUPS = 4
LRU_BATCH_PAIR = 2
ATT_HD = 64
ATT_Q_HEADS = 16
ATT_KV_HEADS = 4
ATT_GROUP = 4
ATT_KV_W = 256
WINDOW = 128
ROPE_BASE = 10000.0

OFF_AQ = 0
OFF_AG = 1024
OFF_RV = 2048
OFF_RG = 3072
OFF_LX = 4096
OFF_LG = 5120
OFF_RQ = 6144
OFF_RK = 6656
OFF_AK = 7168
OFF_AV = 7424
W_IN = 7680

VMEM_LIMIT = 56 * 1024 * 1024


def _cparams(sem):
    return pltpu.CompilerParams(dimension_semantics=sem, vmem_limit_bytes=VMEM_LIMIT)


def _sigmoid(x):
    return 0.5 * jnp.tanh(0.5 * x) + 0.5


def _silu(x):
    return x * _sigmoid(x)


def _softplus(z):
    return jnp.maximum(z, 0.0) + jnp.log1p(jnp.exp(-jnp.abs(z)))


def _dot(a, b):
    return jnp.dot(a, b, preferred_element_type=F32)


def _dot_nt(a, b):
    return lax.dot_general(a, b, (((1,), (1,)), ((), ())), preferred_element_type=F32)


def _dot_tn(a, b):
    return lax.dot_general(a, b, (((0,), (0,)), ((), ())), preferred_element_type=F32)


def _modulated_norm(x, g, mod):
    y = x * lax.rsqrt(jnp.mean(x * x, axis=-1, keepdims=True) + EPS) * g
    return y * (1.0 + mod[:, D:2 * D]) + mod[:, 0:D]


def _ada_kernel(c_ref, w_ref, b_ref, o_ref):
    s = _silu(c_ref[...])
    o_ref[...] = _dot(s.astype(BF16), w_ref[...].astype(BF16)) + b_ref[...]


def _ada_mod(cond8, w_ada, b_ada):
    tn = 1024
    return pl.pallas_call(
        _ada_kernel,
        grid=(DEPTH, 3 * D // tn),
        in_specs=[
            pl.BlockSpec((8, D), lambda l, j: (0, 0)),
            pl.BlockSpec((None, D, tn), lambda l, j: (l, 0, j)),
            pl.BlockSpec((None, 1, tn), lambda l, j: (l, 0, j)),
        ],
        out_specs=pl.BlockSpec((None, 8, tn), lambda l, j: (l, 0, j)),
        out_shape=jax.ShapeDtypeStruct((DEPTH, 8, 3 * D), F32),
        compiler_params=_cparams(("arbitrary", "arbitrary")),
    )(cond8, w_ada, b_ada.reshape(DEPTH, 1, 3 * D))


IN_TM = 1024
IN_TN = 1280


def _in_kernel(x_ref, mod_ref, g_ref, w_ref, o_ref, h_all):
    j = pl.program_id(0)
    i = pl.program_id(1)
    r0 = pl.multiple_of(i * IN_TM, IN_TM)

    @pl.when(j == 0)
    def _():
        h = _modulated_norm(x_ref[...], g_ref[...], mod_ref[...])
        h_all[pl.ds(r0, IN_TM), :] = h.astype(BF16)

    o_ref[...] = _dot(h_all[pl.ds(r0, IN_TM), :], w_ref[...])


def _in_proj(x2d, mod3, row_fn, g_pre, w_in):
    T = x2d.shape[0]
    n_m = T // IN_TM
    return pl.pallas_call(
        _in_kernel,
        grid=(W_IN // IN_TN, n_m),
        in_specs=[
            pl.BlockSpec((IN_TM, D), lambda j, i: (jnp.where(j == 0, i, n_m - 1), 0)),
            pl.BlockSpec((None, 1, 3 * D), lambda j, i: (row_fn(i * IN_TM), 0, 0)),
            pl.BlockSpec((1, D), lambda j, i: (0, 0)),
            pl.BlockSpec((D, IN_TN), lambda j, i: (0, j)),
        ],
        out_specs=pl.BlockSpec((IN_TM, IN_TN), lambda j, i: (i, j)),
        out_shape=jax.ShapeDtypeStruct((T, W_IN), F32),
        scratch_shapes=[pltpu.VMEM((T, D), BF16)],
        compiler_params=_cparams(("arbitrary", "arbitrary")),
    )(x2d, mod3, g_pre, w_in)


def _rope(x, cos_t, sin_t, half):
    lane = lax.broadcasted_iota(jnp.int32, x.shape, 1)
    up = pltpu.roll(x, x.shape[1] - half, axis=1)
    dn = pltpu.roll(x, half, axis=1)
    partner = jnp.where((lane & half) == 0, up, dn)
    return x * cos_t + partner * sin_t


def _rope_tables(n_tokens, dim):
    nq = dim // 4
    lane = np.arange(128)
    within = lane % dim
    axis = within // (2 * nq)
    freq = within % nq
    sign = np.where((within % (2 * nq)) < nq, -1.0, 1.0).astype(np.float32)
    inv = ROPE_BASE ** (-jnp.arange(nq, dtype=F32) / nq)
    t = jnp.arange(n_tokens)
    pos = jnp.stack([(t // GRID_W).astype(F32), (t % GRID_W).astype(F32)], axis=1)
    ang = pos[:, axis] * inv[freq][None, :]
    return jnp.cos(ang), jnp.sin(ang) * sign[None, :]


def _ret_kernel(*refs, n_chunks, has_ctx):
    if has_ctx:
        (q_ref, k_ref, v_ref, g_ref, rd_ref, cos_ref, sin_ref, s0_ref,
         o_ref, s_ref, qs, ks, s_all, sf_run, sb_run) = refs
    else:
        (q_ref, k_ref, v_ref, g_ref, rd_ref,
         o_ref, s_ref, qs, ks, s_all, sf_run, sb_run) = refs
    C = CHUNK

    q = q_ref[...] * (RET_DK ** -0.5)
    k = k_ref[...]
    if has_ctx:
        q = _rope(q, cos_ref[...], sin_ref[...], 32)
        k = _rope(k, cos_ref[...], sin_ref[...], 32)
        sf_run[...] = s0_ref[0]
        sb_run[...] = s0_ref[1]
    else:
        sf_run[...] = jnp.zeros_like(sf_run)
        sb_run[...] = jnp.zeros_like(sb_run)
    qs[...] = q
    ks[...] = k

    lg_f = -_softplus(-rd_ref[0])
    lg_b = -_softplus(-rd_ref[1])
    lgf = lg_f[:, :C]
    lgb = lg_b[:, :C]
    ii = lax.broadcasted_iota(jnp.int32, (C, C), 0)
    jj = lax.broadcasted_iota(jnp.int32, (C, C), 1)
    diff = (ii - jj).astype(F32)
    dmat = (jnp.where(diff >= 0, jnp.exp(lgf * jnp.maximum(diff, 0.0)), 0.0)
            + jnp.where(diff <= 0, jnp.exp(lgb * jnp.maximum(-diff, 0.0)), 0.0))
    ri = ii.astype(F32)
    dq_f = jnp.exp(lgf * (ri + 1.0))
    dk_f = jnp.exp(lgf * (C - 1.0 - ri))
    dq_b = jnp.exp(lgb * (C - ri))
    dk_b = jnp.exp(lgb * ri)
    dc_f = jnp.exp(lg_f * float(C))
    dc_b = jnp.exp(lg_b * float(C))

    def state_step(t, carry):
        cf = t
        cb = n_chunks - 1 - t
        rf = pl.multiple_of(cf * C, C)
        rb = pl.multiple_of(cb * C, C)
        sf = sf_run[...]
        sb = sb_run[...]
        s_all[cf, 0:RET_DK, :] = sf.astype(BF16)
        s_all[cb, RET_DK:2 * RET_DK, :] = sb.astype(BF16)
        kf = (ks[pl.ds(rf, C), :] * dk_f).astype(BF16)
        kb = (ks[pl.ds(rb, C), :] * dk_b).astype(BF16)
        sf_run[...] = sf * dc_f + _dot_tn(kf, v_ref[pl.ds(rf, C), :].astype(BF16))
        sb_run[...] = sb * dc_b + _dot_tn(kb, v_ref[pl.ds(rb, C), :].astype(BF16))
        return carry

    def out_step(c, carry):
        r0 = pl.multiple_of(c * C, C)
        qc = qs[pl.ds(r0, C), :]
        kc = ks[pl.ds(r0, C), :]
        vc = v_ref[pl.ds(r0, C), :].astype(BF16)
        a = _dot_nt(qc.astype(BF16), kc.astype(BF16)) * dmat
        qq = jnp.concatenate([qc * dq_f, qc * dq_b], axis=1).astype(BF16)
        o = _dot(a.astype(BF16), vc) + _dot(qq, s_all[c])
        mu = jnp.mean(o, axis=-1, keepdims=True)
        oc = o - mu
        on = oc * lax.rsqrt(jnp.mean(oc * oc, axis=-1, keepdims=True) + EPS)
        o_ref[pl.ds(r0, C), :] = on * _silu(g_ref[pl.ds(r0, C), :])
        return carry

    if n_chunks <= 2:
        for t in range(n_chunks):
            state_step(t, 0)
        for c in range(n_chunks):
            out_step(c, 0)
    else:
        lax.fori_loop(0, n_chunks, state_step, 0)
        lax.fori_loop(0, n_chunks, out_step, 0, unroll=2)
    s_ref[0] = sf_run[...]
    s_ref[1] = sb_run[...]


def _retention(pm3, rd4, ropes, s0):
    B, L, _ = pm3.shape
    n_chunks = L // CHUNK
    has_ctx = s0 is not None
    H = RET_HEADS
    in_specs = [
        pl.BlockSpec((None, L, RET_DK), lambda b, h: (b, 0, OFF_RQ // RET_DK + h)),
        pl.BlockSpec((None, L, RET_DK), lambda b, h: (b, 0, OFF_RK // RET_DK + h)),
        pl.BlockSpec((None, L, RET_DV), lambda b, h: (b, 0, OFF_RV // RET_DV + h)),
        pl.BlockSpec((None, L, RET_DV), lambda b, h: (b, 0, OFF_RG // RET_DV + h)),
        pl.BlockSpec((2, None, 1, RET_DV), lambda b, h: (0, h, 0, 0)),
    ]
    args = [pm3, pm3, pm3, pm3, rd4]
    if has_ctx:
        in_specs += [
            pl.BlockSpec((L, 128), lambda b, h: (0, 0)),
            pl.BlockSpec((L, 128), lambda b, h: (0, 0)),
            pl.BlockSpec((None, 2, None, RET_DK, RET_DV), lambda b, h: (b, 0, h, 0, 0)),
        ]
        args += [ropes[0], ropes[1], s0]
    return pl.pallas_call(
        functools.partial(_ret_kernel, n_chunks=n_chunks, has_ctx=has_ctx),
        grid=(B, H),
        in_specs=in_specs,
        out_specs=[
            pl.BlockSpec((None, L, RET_DV), lambda b, h: (b, 0, h)),
            pl.BlockSpec((None, 2, None, RET_DK, RET_DV), lambda b, h: (b, 0, h, 0, 0)),
        ],
        out_shape=[
            jax.ShapeDtypeStruct((B, L, D), F32),
            jax.ShapeDtypeStruct((B, 2, H, RET_DK, RET_DV), F32),
        ],
        scratch_shapes=[
            pltpu.VMEM((L, RET_DK), F32),
            pltpu.VMEM((L, RET_DK), F32),
            pltpu.VMEM((n_chunks, 2 * RET_DK, RET_DV), BF16),
            pltpu.VMEM((RET_DK, RET_DV), F32),
            pltpu.VMEM((RET_DK, RET_DV), F32),
        ],
        compiler_params=_cparams(("arbitrary", "arbitrary")),
    )(*args)


def _lru_kernel(*refs, L, has_ctx):
    if has_ctx:
        (x_ref, gt_ref, cw_ref, cb_ref, w_ref, b_ref, lam_ref, h0_ref,
         o_ref, s_ref, xc_scr, hf_scr, pb_scr, ub_scr) = refs
    else:
        (x_ref, gt_ref, cw_ref, cb_ref, w_ref, b_ref, lam_ref,
         o_ref, s_ref, xc_scr, hf_scr, pb_scr, ub_scr) = refs
    G = LRU_GROUP
    NG = LRU_ITER_GROUPS
    n_it = L // (NG * G)
    W = LRU_BS
    BB = x_ref.shape[0]

    row = lax.broadcasted_iota(jnp.int32, (L, W), 0)
    for bb in range(BB):
        x = x_ref[bb]
        xm1 = jnp.where(row >= 1, pltpu.roll(x, 1, axis=0), 0.0)
        xp1 = jnp.where(row < L - 1, pltpu.roll(x, L - 1, axis=0), 0.0)
        xp2 = jnp.where(row < L - 2, pltpu.roll(x, L - 2, axis=0), 0.0)
        xc_scr[bb] = (cw_ref[0:1, :] * xm1 + cw_ref[1:2, :] * x + cw_ref[2:3, :] * xp1
                      + cw_ref[3:4, :] * xp2 + cb_ref[...])

    nsp_f = -LRU_C * _softplus(-lam_ref[0:1, :])
    nsp_b = -LRU_C * _softplus(-lam_ref[1:2, :])
    row8 = lax.broadcasted_iota(jnp.int32, (8, W), 0)

    def gates(zr, zi, nsp, xc):
        log_a = _sigmoid(zr) * nsp
        a = jnp.exp(log_a)
        u = jnp.sqrt(-jnp.tanh(log_a) * (a * a + 1.0)) * (_sigmoid(zi) * xc)
        return a, u

    def seg_scan(p, hl, carry, reverse):
        a, u = p, hl
        for s in (1, 2, 4):
            sh = (8 - s) if reverse else s
            a_sh = pltpu.roll(a, sh, axis=0)
            u_sh = pltpu.roll(u, sh, axis=0)
            m = (row8 < 8 - s) if reverse else (row8 >= s)
            u = jnp.where(m, a * u_sh + u, u)
            a = jnp.where(m, a * a_sh, a)
        h_end = u + a * carry
        if reverse:
            h_in = jnp.where(row8 == 7, carry, pltpu.roll(h_end, 7, axis=0))
            return h_in, h_end[0:1]
        h_in = jnp.where(row8 == 0, carry, pltpu.roll(h_end, 1, axis=0))
        return h_in, h_end[7:8]

    def pass1(it, carries):
        return tuple(pass1_one(it, carries[bb], xc_scr.at[bb], hf_scr.at[bb], pb_scr.at[bb], ub_scr.at[bb])
                     for bb in range(BB))

    def pass2(t, carries):
        return tuple(pass2_one(t, carries[bb], gt_ref.at[bb], o_ref.at[bb], hf_scr.at[bb], pb_scr.at[bb],
                               ub_scr.at[bb]) for bb in range(BB))

    def pass1_one(it, carry, xc_scr, hf_scr, pb_scr, ub_scr):
        base = pl.multiple_of(it * (NG * G), NG * G)
        xg = jnp.concatenate([xc_scr[pl.ds(base + gi * G + r, 8, stride=8), :]
                              for gi in range(NG) for r in range(8)], axis=0)
        zz = _dot(xg.astype(BF16), w_ref[...]) + b_ref[...]
        a_f, u_f = gates(zz[:, 0:W], zz[:, W:2 * W], nsp_f, xg)
        a_b, u_b = gates(zz[:, 2 * W:3 * W], zz[:, 3 * W:4 * W], nsp_b, xg)
        for gi in range(NG):
            v = lambda arr, r: arr[gi * G + r * 8:gi * G + (r + 1) * 8]
            p = [v(a_f, 0)]
            hl = [v(u_f, 0)]
            for r in range(1, 8):
                hl.append(v(a_f, r) * hl[-1] + v(u_f, r))
                p.append(v(a_f, r) * p[-1])
            h_in, carry = seg_scan(p[7], hl[7], carry, False)
            hf_scr[pl.ds(base + gi * G, G), :] = jnp.concatenate(
                [hl[r] + p[r] * h_in for r in range(8)], axis=0)
            pb = [None] * 8
            hb = [None] * 8
            pb[7] = v(a_b, 7)
            hb[7] = v(u_b, 7)
            for r in range(6, -1, -1):
                hb[r] = v(a_b, r) * hb[r + 1] + v(u_b, r)
                pb[r] = v(a_b, r) * pb[r + 1]
            pb_scr[pl.ds(base + gi * G, G), :] = jnp.concatenate(pb, axis=0)
            ub_scr[pl.ds(base + gi * G, G), :] = jnp.concatenate(hb, axis=0)
        return carry

    def pass2_one(t, carry, gt_ref, o_ref, hf_scr, pb_scr, ub_scr):
        base = pl.multiple_of((n_it - 1 - t) * (NG * G), NG * G)
        for gi in range(NG - 1, -1, -1):
            gb = base + gi * G
            pb = pb_scr[pl.ds(gb, G), :]
            hb = ub_scr[pl.ds(gb, G), :]
            hf = hf_scr[pl.ds(gb, G), :]
            h_in, carry = seg_scan(pb[0:8], hb[0:8], carry, True)
            for r in range(8):
                sl = slice(r * 8, (r + 1) * 8)
                gt = gt_ref[pl.ds(gb + r, 8, stride=8), :]
                o_ref[pl.ds(gb + r, 8, stride=8), :] = (hf[sl] + hb[sl] + pb[sl] * h_in) * _silu(gt)
        return carry

    if has_ctx:
        h0f = tuple(h0_ref[bb, 0:1, :] for bb in range(BB))
        h0b = tuple(h0_ref[bb, 1:2, :] for bb in range(BB))
    else:
        h0f = h0b = tuple(jnp.zeros((1, W), F32) for _ in range(BB))
    if n_it == 1:
        lf = pass1(0, h0f)
        lb = pass2(0, h0b)
    else:
        lf = lax.fori_loop(0, n_it, pass1, h0f)
        lb = lax.fori_loop(0, n_it, pass2, h0b)
    for bb in range(BB):
        s_ref[bb, 0:1, :] = lf[bb]
        s_ref[bb, 1:2, :] = lb[bb]


def _rglru(pm3, conv_w, conv_b, w_gate, b_gate, lam, h0):
    B, L, _ = pm3.shape
    has_ctx = h0 is not None
    W = LRU_BS
    BB = LRU_BATCH_PAIR
    in_specs = [
        pl.BlockSpec((BB, L, W), lambda b, n: (b, 0, OFF_LX // W + n)),
        pl.BlockSpec((BB, L, W), lambda b, n: (b, 0, OFF_LG // W + n)),
        pl.BlockSpec((4, W), lambda b, n: (0, n)),
        pl.BlockSpec((1, W), lambda b, n: (0, n)),
        pl.BlockSpec((None, W, 4 * W), lambda b, n: (n, 0, 0)),
        pl.BlockSpec((None, 1, 4 * W), lambda b, n: (n, 0, 0)),
        pl.BlockSpec((2, W), lambda b, n: (0, n)),
    ]
    args = [pm3, pm3, conv_w, conv_b, w_gate, b_gate, lam]
    if has_ctx:
        in_specs.append(pl.BlockSpec((BB, 2, W), lambda b, n: (b, 0, n)))
        args.append(h0)
    return pl.pallas_call(
        functools.partial(_lru_kernel, L=L, has_ctx=has_ctx),
        grid=(B // BB, LRU_BLOCKS),
        in_specs=in_specs,
        out_specs=[
            pl.BlockSpec((BB, L, W), lambda b, n: (b, 0, n)),
            pl.BlockSpec((BB, 2, W), lambda b, n: (b, 0, n)),
        ],
        out_shape=[
            jax.ShapeDtypeStruct((B, L, D), F32),
            jax.ShapeDtypeStruct((B, 2, D), F32),
        ],
        scratch_shapes=[pltpu.VMEM((BB, L, W), F32)] * 4,
        compiler_params=_cparams(("arbitrary", "arbitrary")),
    )(*args)


def _att_kernel(*refs, L, has_ctx):
    if has_ctx:
        (sink_ref, q_ref, ag_ref, k_ref, v_ref, ck_ref, cv_ref, cosq_ref, sinq_ref, cosk_ref, sink_tab_ref,
         o_ref, km, vm, ckm, cvm) = refs
    else:
        (sink_ref, q_ref, ag_ref, k_ref, v_ref, o_ref, km, vm) = refs
    Q = CHUNK
    G = ATT_GROUP
    KW = ATT_KV_W
    qi = pl.program_id(1)

    def head_masked(dst, val):
        head = lax.shift_right_logical(lax.broadcasted_iota(jnp.int32, val.shape, 1), 6)
        for h in range(ATT_KV_HEADS):
            dst[h] = jnp.where(head == h, val, 0.0).astype(BF16)

    @pl.when(qi == 0)
    def _():
        kk = k_ref[...]
        if has_ctx:
            kk = _rope(kk, jnp.concatenate([cosk_ref[...]] * 2, axis=1),
                       jnp.concatenate([sink_tab_ref[...]] * 2, axis=1), 16)
            head_masked(ckm, ck_ref[...])
            head_masked(cvm, cv_ref[...])
        head_masked(km, kk)
        head_masked(vm, v_ref[...])

    q = q_ref[...] * (ATT_HD ** -0.5)
    if has_ctx:
        q = _rope(q, jnp.concatenate([cosq_ref[...]] * 8, axis=1),
                  jnp.concatenate([sinq_ref[...]] * 8, axis=1), 16)
        span = Q + 2 * WINDOW
        start = pl.multiple_of(jnp.clip((qi - 1) * Q, 0, L - span), Q)
        qpos = qi * Q + lax.broadcasted_iota(jnp.int32, (Q, span), 0)
        kpos = start + lax.broadcasted_iota(jnp.int32, (Q, span), 1)
        band = jnp.abs(qpos - kpos) <= WINDOW
        band4 = jnp.concatenate([band] * G, axis=0)
    qg = jnp.concatenate([q[:, g * KW:(g + 1) * KW] for g in range(G)], axis=0).astype(BF16)

    o = None
    for h in range(ATT_KV_HEADS):
        sk = jnp.concatenate(
            [jnp.full((Q, 1), sink_ref[h * G + g], F32) for g in range(G)], axis=0)
        if has_ctx:
            s = jnp.concatenate([jnp.where(band4, _dot_nt(qg, km[h, pl.ds(start, span), :]), -jnp.inf),
                                 _dot_nt(qg, ckm[h])], axis=1)
            m = jnp.maximum(jnp.max(s, axis=-1, keepdims=True), sk)
            p = jnp.exp(s - m)
            den = jnp.sum(p, axis=-1, keepdims=True) + jnp.exp(sk - m)
            pb = p.astype(BF16)
            oh = _dot(pb[:, :span], vm[h, pl.ds(start, span), :]) + _dot(pb[:, span:], cvm[h])
        else:
            s1 = _dot_nt(qg, km[h])
            m = jnp.maximum(jnp.max(s1, axis=-1, keepdims=True), sk)
            p1 = jnp.exp(s1 - m)
            den = jnp.sum(p1, axis=-1, keepdims=True) + jnp.exp(sk - m)
            oh = _dot(p1.astype(BF16), vm[h])
        oh = oh * (1.0 / den)
        o = oh if o is None else o + oh
    ao = jnp.concatenate([o[g * Q:(g + 1) * Q] for g in range(G)], axis=1)
    o_ref[...] = ao * _silu(ag_ref[...])


def _attention(pm3, sink, ropes, ck, cv):
    B, L, _ = pm3.shape
    has_ctx = ck is not None
    Q = CHUNK
    KW = ATT_KV_W
    in_specs = [
        pl.BlockSpec(memory_space=pltpu.SMEM),
        pl.BlockSpec((None, Q, D), lambda b, i: (b, i, OFF_AQ // D)),
        pl.BlockSpec((None, Q, D), lambda b, i: (b, i, OFF_AG // D)),
        pl.BlockSpec((None, L, KW), lambda b, i: (b, 0, OFF_AK // KW)),
        pl.BlockSpec((None, L, KW), lambda b, i: (b, 0, OFF_AV // KW)),
    ]
    args = [sink, pm3, pm3, pm3, pm3]
    scratch = [pltpu.VMEM((ATT_KV_HEADS, L, KW), BF16), pltpu.VMEM((ATT_KV_HEADS, L, KW), BF16)]
    if has_ctx:
        P = ck.shape[1]
        in_specs += [
            pl.BlockSpec((None, P, KW), lambda b, i: (b, 0, 0)),
            pl.BlockSpec((None, P, KW), lambda b, i: (b, 0, 0)),
            pl.BlockSpec((Q, 128), lambda b, i: (i, 0)),
            pl.BlockSpec((Q, 128), lambda b, i: (i, 0)),
            pl.BlockSpec((L, 128), lambda b, i: (0, 0)),
            pl.BlockSpec((L, 128), lambda b, i: (0, 0)),
        ]
        args += [ck, cv, ropes[0], ropes[1], ropes[0], ropes[1]]
        scratch += [pltpu.VMEM((ATT_KV_HEADS, P, KW), BF16), pltpu.VMEM((ATT_KV_HEADS, P, KW), BF16)]
    return pl.pallas_call(
        functools.partial(_att_kernel, L=L, has_ctx=has_ctx),
        grid=(B, L // Q),
        in_specs=in_specs,
        out_specs=pl.BlockSpec((None, Q, D), lambda b, i: (b, i, 0)),
        out_shape=jax.ShapeDtypeStruct((B, L, D), F32),
        scratch_shapes=scratch,
        compiler_params=_cparams(("arbitrary", "arbitrary")),
    )(*args)


OUT_TM = 512


def _out_kernel(x_ref, ro_ref, lo_ref, ao_ref, mod_ref, gpre_ref, gpost_ref,
                wm_ref, bm_ref, wb_ref, wo_ref, y_ref):
    x = x_ref[...]
    mod = mod_ref[...]
    h = _modulated_norm(x, gpre_ref[...], mod).astype(BF16)
    merged = None
    for n, br_ref in enumerate((ro_ref, lo_ref, ao_ref)):
        gate = _sigmoid(_dot(h, wm_ref[:, n * D:(n + 1) * D]) + bm_ref[:, n * D:(n + 1) * D])
        z = gate * _dot(br_ref[...].astype(BF16), wb_ref[n])
        merged = z if merged is None else merged + z
    out = _dot(merged.astype(BF16), wo_ref[...])
    nrm = out * lax.rsqrt(jnp.mean(out * out, axis=-1, keepdims=True) + EPS) * gpost_ref[...]
    y_ref[...] = x + mod[:, 2 * D:3 * D] * nrm


def _merge_out(x2d, ro, lo, ao, mod3, row_fn, g_pre, g_post, w_merge, b_merge, w_branch, w_out):
    T = x2d.shape[0]
    tm = OUT_TM
    tok = lambda i: (i, 0)
    once = pl.Buffered(1)
    return pl.pallas_call(
        _out_kernel,
        grid=(T // tm,),
        in_specs=[
            pl.BlockSpec((tm, D), tok),
            pl.BlockSpec((tm, D), tok),
            pl.BlockSpec((tm, D), tok),
            pl.BlockSpec((tm, D), tok),
            pl.BlockSpec((None, 1, 3 * D), lambda i: (row_fn(i * tm), 0, 0)),
            pl.BlockSpec((1, D), lambda i: (0, 0)),
            pl.BlockSpec((1, D), lambda i: (0, 0)),
            pl.BlockSpec((D, N_BRANCH * D), lambda i: (0, 0), pipeline_mode=once),
            pl.BlockSpec((1, N_BRANCH * D), lambda i: (0, 0)),
            pl.BlockSpec((N_BRANCH, D, D), lambda i: (0, 0, 0), pipeline_mode=once),
            pl.BlockSpec((D, D), lambda i: (0, 0), pipeline_mode=once),
        ],
        out_specs=pl.BlockSpec((tm, D), tok),
        out_shape=jax.ShapeDtypeStruct((T, D), F32),
        compiler_params=_cparams(("arbitrary",)),
    )(x2d, ro, lo, ao, mod3, g_pre, g_post, w_merge, b_merge, w_branch, w_out)


def _layer(x, mod3, row_of_token, lw, ctx, ropes_ret, ropes_att):
    B, L, _ = x.shape
    x2d = x.reshape(B * L, D)
    pm2 = _in_proj(x2d, mod3, row_of_token, lw['g_pre'], lw['w_in'])
    pm3 = pm2.reshape(B, L, W_IN)
    if ctx is None:
        ro, ret_s = _retention(pm3, lw['rd4'], None, None)
        lo, lru_s = _rglru(pm3, lw['conv_w'], lw['conv_b'], lw['w_gate'], lw['b_gate'], lw['lam'], None)
        ao = _attention(pm3, lw['sink'], None, None, None)
    else:
        s_ret, s_lru, ck, cv = ctx
        ro, ret_s = _retention(pm3, lw['rd4'], ropes_ret, s_ret)
        lo, lru_s = _rglru(pm3, lw['conv_w'], lw['conv_b'], lw['w_gate'], lw['b_gate'], lw['lam'], s_lru)
        ao = _attention(pm3, lw['sink'], ropes_att, ck, cv)
    y = _merge_out(x2d, ro.reshape(B * L, D), lo.reshape(B * L, D), ao.reshape(B * L, D),
                   mod3, row_of_token, lw['g_pre'], lw['g_post'], lw['w_merge'], lw['b_merge'],
                   lw['w_branch'], lw['w_out'])
    return y.reshape(B, L, D), pm3, ret_s, lru_s


def _group_major(w, axis):
    shp = w.shape
    w = w.reshape(shp[:axis] + (ATT_KV_HEADS, ATT_GROUP, ATT_HD) + shp[axis + 1:])
    w = jnp.swapaxes(w, axis, axis + 1)
    return w.reshape(shp)


def _layer_weights(l, norm_pre, norm_post, w_in, ret_decay, conv_w, conv_b, lru_wa, lru_ba, lru_wx,
                   lru_bx, lru_lambda, att_sink, w_branch, w_merge, b_merge, w_out):
    offs = np.cumsum((512, 512, 1024, 1024, 1024, 1024, 1024, 256, 256, 1024))[:-1].tolist()
    rq, rk, rv, rg, lx, lgt, aq, ak, av, ag = jnp.split(w_in[l], offs, axis=-1)
    w_cat = jnp.concatenate([_group_major(aq, 1), _group_major(ag, 1), rv, rg, lx, lgt, rq, rk, ak, av],
                            axis=-1).astype(BF16)
    wb = jnp.stack([w_branch[l, 0], w_branch[l, 1], _group_major(w_branch[l, 2], 0)], axis=0).astype(BF16)
    w_gate = jnp.concatenate([lru_wa[l, 0], lru_wx[l, 0], lru_wa[l, 1], lru_wx[l, 1]], axis=-1).astype(BF16)
    b4 = jnp.stack([lru_ba[l, 0], lru_bx[l, 0], lru_ba[l, 1], lru_bx[l, 1]], axis=0)
    b_gate = b4.reshape(4, LRU_BLOCKS, LRU_BS).transpose(1, 0, 2).reshape(LRU_BLOCKS, 1, 4 * LRU_BS)
    rd4 = jnp.broadcast_to(ret_decay[l][:, :, None, None], (2, RET_HEADS, 1, RET_DV))
    return {
        'g_pre': norm_pre[l].reshape(1, D), 'g_post': norm_post[l].reshape(1, D),
        'w_in': w_cat, 'rd4': rd4,
        'conv_w': conv_w[l], 'conv_b': conv_b[l].reshape(1, D),
        'w_gate': w_gate, 'b_gate': b_gate, 'lam': lru_lambda[l],
        'sink': att_sink[l], 'w_merge': w_merge[l].astype(BF16), 'b_merge': b_merge[l].reshape(1, N_BRANCH * D),
        'w_branch': wb, 'w_out': w_out[l].astype(BF16),
    }


def kernel(x_prompt, x_sample, cache_k, cache_v, state_ret, state_lru, c, c_ctx, w_ada, b_ada, norm_pre, norm_post, w_in, ret_decay, conv_w, conv_b, lru_wa, lru_ba, lru_wx, lru_bx, lru_lambda, att_sink, w_branch, w_merge, b_merge, w_out):
    B, S, _ = x_prompt.shape
    Bd, Ld, _ = x_sample.shape
    P = cache_k.shape[2]
    cond8 = jnp.concatenate([c_ctx[None, :], c, jnp.zeros((8 - 1 - Bd, D), F32)], axis=0)
    mod = _ada_mod(cond8, w_ada, b_ada)
    ropes_ret = _rope_tables(Ld, RET_DK)
    ropes_att = _rope_tables(Ld, ATT_HD)

    yp, ys = x_prompt, x_sample
    new_k, new_v, new_ret, new_lru = [], [], [], []
    for l in range(DEPTH):
        lw = _layer_weights(l, norm_pre, norm_post, w_in, ret_decay, conv_w, conv_b, lru_wa, lru_ba,
                            lru_wx, lru_bx, lru_lambda, att_sink, w_branch, w_merge, b_merge, w_out)
        mod3 = mod[l].reshape(8, 1, 3 * D)
        yp, pm_p, r_s, l_s = _layer(yp, mod3, lambda t: 0, lw, None, None, None)
        new_ret.append(r_s)
        new_lru.append(l_s)
        new_k.append(pm_p[:, :, OFF_AK:OFF_AK + ATT_KV_W].reshape(B, S, ATT_KV_HEADS, ATT_HD))
        new_v.append(pm_p[:, :, OFF_AV:OFF_AV + ATT_KV_W].reshape(B, S, ATT_KV_HEADS, ATT_HD))
        ctx = (state_ret[:, l], state_lru[:, l],
               cache_k[:, l].reshape(Bd, P, ATT_KV_W), cache_v[:, l].reshape(Bd, P, ATT_KV_W))
        ys, _, _, _ = _layer(ys, mod3, lambda t: 1 + t // Ld, lw, ctx, ropes_ret, ropes_att)
    return (yp, ys, jnp.stack(new_k, axis=1), jnp.stack(new_v, axis=1),
            jnp.stack(new_ret, axis=1), jnp.stack(new_lru, axis=1))
```

```python
import functools

import numpy as np
import jax
import jax.numpy as jnp
from jax import lax
from jax.experimental import pallas as pl
from jax.experimental.pallas import tpu as pltpu

F32 = jnp.float32
BF16 = jnp.bfloat16

D = 1024
DEPTH = 2
GRID_W = 64
EPS = 1e-6
N_BRANCH = 3
RET_HEADS = 4
RET_DV = 256
RET_DK = 128
CHUNK = 128
LRU_BLOCKS = 8
LRU_BS = 128
LRU_C = 8.0
LRU_GROUP = 64
LRU_ITER_GROUPS = 4
LRU_BATCH_PAIR = 2
ATT_HD = 64
ATT_Q_HEADS = 16
ATT_KV_HEADS = 4
ATT_GROUP = 4
ATT_KV_W = 256
WINDOW = 128
ROPE_BASE = 10000.0

OFF_AQ = 0
OFF_AG = 1024
OFF_RV = 2048
OFF_RG = 3072
OFF_LX = 4096
OFF_LG = 5120
OFF_RQ = 6144
OFF_RK = 6656
OFF_AK = 7168
OFF_AV = 7424
W_IN = 7680

VMEM_LIMIT = 56 * 1024 * 1024


def _cparams(sem):
    return pltpu.CompilerParams(dimension_semantics=sem, vmem_limit_bytes=VMEM_LIMIT)


def _sigmoid(x):
    return 0.5 * jnp.tanh(0.5 * x) + 0.5


def _silu(x):
    return x * _sigmoid(x)


def _softplus(z):
    return jnp.maximum(z, 0.0) + jnp.log1p(jnp.exp(-jnp.abs(z)))


def _dot(a, b):
    return jnp.dot(a, b, preferred_element_type=F32)


def _dot_nt(a, b):
    return lax.dot_general(a, b, (((1,), (1,)), ((), ())), preferred_element_type=F32)


def _dot_tn(a, b):
    return lax.dot_general(a, b, (((0,), (0,)), ((), ())), preferred_element_type=F32)


def _modulated_norm(x, g, mod):
    y = x * lax.rsqrt(jnp.mean(x * x, axis=-1, keepdims=True) + EPS) * g
    return y * (1.0 + mod[:, D:2 * D]) + mod[:, 0:D]


def _ada_kernel(c_ref, w_ref, b_ref, o_ref):
    s = _silu(c_ref[...])
    o_ref[...] = _dot(s.astype(BF16), w_ref[...].astype(BF16)) + b_ref[...]


def _ada_mod(cond8, w_ada, b_ada):
    tn = 1024
    return pl.pallas_call(
        _ada_kernel,
        grid=(DEPTH, 3 * D // tn),
        in_specs=[
            pl.BlockSpec((8, D), lambda l, j: (0, 0)),
            pl.BlockSpec((None, D, tn), lambda l, j: (l, 0, j)),
            pl.BlockSpec((None, 1, tn), lambda l, j: (l, 0, j)),
        ],
        out_specs=pl.BlockSpec((None, 8, tn), lambda l, j: (l, 0, j)),
        out_shape=jax.ShapeDtypeStruct((DEPTH, 8, 3 * D), F32),
        compiler_params=_cparams(("arbitrary", "arbitrary")),
    )(cond8, w_ada, b_ada.reshape(DEPTH, 1, 3 * D))


IN_TM = 1024
IN_TN = 1280


def _in_kernel(x_ref, mod_ref, g_ref, w_ref, o_ref, h_all):
    j = pl.program_id(0)
    i = pl.program_id(1)
    r0 = pl.multiple_of(i * IN_TM, IN_TM)

    @pl.when(j == 0)
    def _():
        h = _modulated_norm(x_ref[...], g_ref[...], mod_ref[...])
        h_all[pl.ds(r0, IN_TM), :] = h.astype(BF16)

    o_ref[...] = _dot(h_all[pl.ds(r0, IN_TM), :], w_ref[...])


def _in_proj(x2d, mod3, row_fn, g_pre, w_in):
    T = x2d.shape[0]
    n_m = T // IN_TM
    return pl.pallas_call(
        _in_kernel,
        grid=(W_IN // IN_TN, n_m),
        in_specs=[
            pl.BlockSpec((IN_TM, D), lambda j, i: (jnp.where(j == 0, i, n_m - 1), 0)),
            pl.BlockSpec((None, 1, 3 * D), lambda j, i: (row_fn(i * IN_TM), 0, 0)),
            pl.BlockSpec((1, D), lambda j, i: (0, 0)),
            pl.BlockSpec((D, IN_TN), lambda j, i: (0, j)),
        ],
        out_specs=pl.BlockSpec((IN_TM, IN_TN), lambda j, i: (i, j)),
        out_shape=jax.ShapeDtypeStruct((T, W_IN), F32),
        scratch_shapes=[pltpu.VMEM((T, D), BF16)],
        compiler_params=_cparams(("arbitrary", "arbitrary")),
    )(x2d, mod3, g_pre, w_in)


def _rope(x, cos_t, sin_t, half):
    lane = lax.broadcasted_iota(jnp.int32, x.shape, 1)
    up = pltpu.roll(x, x.shape[1] - half, axis=1)
    dn = pltpu.roll(x, half, axis=1)
    partner = jnp.where((lane & half) == 0, up, dn)
    return x * cos_t + partner * sin_t


def _rope_tables(n_tokens, dim):
    nq = dim // 4
    lane = np.arange(128)
    within = lane % dim
    axis = within // (2 * nq)
    freq = within % nq
    sign = np.where((within % (2 * nq)) < nq, -1.0, 1.0).astype(np.float32)
    inv = ROPE_BASE ** (-jnp.arange(nq, dtype=F32) / nq)
    t = jnp.arange(n_tokens)
    pos = jnp.stack([(t // GRID_W).astype(F32), (t % GRID_W).astype(F32)], axis=1)
    ang = pos[:, axis] * inv[freq][None, :]
    return jnp.cos(ang), jnp.sin(ang) * sign[None, :]


def _ret_kernel(*refs, n_chunks, has_ctx, HB):
    if has_ctx:
        (q_ref, k_ref, v_ref, g_ref, rd_ref, cos_ref, sin_ref, s0_ref,
         o_ref, s_ref, qs, ks, s_all, sf_run, sb_run) = refs
    else:
        (q_ref, k_ref, v_ref, g_ref, rd_ref,
         o_ref, s_ref, qs, ks, s_all, sf_run, sb_run) = refs
    C = CHUNK
    ii = lax.broadcasted_iota(jnp.int32, (C, C), 0)
    jj = lax.broadcasted_iota(jnp.int32, (C, C), 1)
    diff = (ii - jj).astype(F32)
    ri = ii.astype(F32)

    for hh in range(HB):
        ksl = slice(hh * RET_DK, (hh + 1) * RET_DK)
        vsl = slice(hh * RET_DV, (hh + 1) * RET_DV)

        q = q_ref[:, ksl] * (RET_DK ** -0.5)
        k = k_ref[:, ksl]
        if has_ctx:
            q = _rope(q, cos_ref[...], sin_ref[...], 32)
            k = _rope(k, cos_ref[...], sin_ref[...], 32)
            sf_run[hh] = s0_ref[0, hh]
            sb_run[hh] = s0_ref[1, hh]
        else:
            sf_run[hh] = jnp.zeros((RET_DK, RET_DV), F32)
            sb_run[hh] = jnp.zeros((RET_DK, RET_DV), F32)
        qs[hh] = q
        ks[hh] = k

        lg_f = -_softplus(-rd_ref[0, hh])
        lg_b = -_softplus(-rd_ref[1, hh])
        lgf = lg_f[:, :C]
        lgb = lg_b[:, :C]
        dmat = (jnp.where(diff >= 0, jnp.exp(lgf * jnp.maximum(diff, 0.0)), 0.0)
                + jnp.where(diff <= 0, jnp.exp(lgb * jnp.maximum(-diff, 0.0)), 0.0))
        dq_f = jnp.exp(lgf * (ri + 1.0))
        dk_f = jnp.exp(lgf * (C - 1.0 - ri))
        dq_b = jnp.exp(lgb * (C - ri))
        dk_b = jnp.exp(lgb * ri)
        dc_f = jnp.exp(lg_f * float(C))
        dc_b = jnp.exp(lg_b * float(C))

        def state_step(t, carry, hh=hh, vsl=vsl, dk_f=dk_f, dk_b=dk_b, dc_f=dc_f, dc_b=dc_b):
            cf = t
            cb = n_chunks - 1 - t
            rf = pl.multiple_of(cf * C, C)
            rb = pl.multiple_of(cb * C, C)
            sf = sf_run[hh]
            sb = sb_run[hh]
            s_all[hh, cf, 0:RET_DK, :] = sf.astype(BF16)
            s_all[hh, cb, RET_DK:2 * RET_DK, :] = sb.astype(BF16)
            kf = (ks[hh, pl.ds(rf, C), :] * dk_f).astype(BF16)
            kb = (ks[hh, pl.ds(rb, C), :] * dk_b).astype(BF16)
            sf_run[hh] = sf * dc_f + _dot_tn(kf, v_ref[pl.ds(rf, C), vsl].astype(BF16))
            sb_run[hh] = sb * dc_b + _dot_tn(kb, v_ref[pl.ds(rb, C), vsl].astype(BF16))
            return carry

        def out_step(c, carry, hh=hh, vsl=vsl, dmat=dmat, dq_f=dq_f, dq_b=dq_b):
            r0 = pl.multiple_of(c * C, C)
            qc = qs[hh, pl.ds(r0, C), :]
            kc = ks[hh, pl.ds(r0, C), :]
            vc = v_ref[pl.ds(r0, C), vsl].astype(BF16)
            a = _dot_nt(qc.astype(BF16), kc.astype(BF16)) * dmat
            qq = jnp.concatenate([qc * dq_f, qc * dq_b], axis=1).astype(BF16)
            o = _dot(a.astype(BF16), vc) + _dot(qq, s_all[hh, c])
            mu = jnp.mean(o, axis=-1, keepdims=True)
            oc = o - mu
            on = oc * lax.rsqrt(jnp.mean(oc * oc, axis=-1, keepdims=True) + EPS)
            o_ref[pl.ds(r0, C), vsl] = on * _silu(g_ref[pl.ds(r0, C), vsl])
            return carry

        if n_chunks <= 2:
            for t in range(n_chunks):
                state_step(t, 0)
            for c in range(n_chunks):
                out_step(c, 0)
        else:
            lax.fori_loop(0, n_chunks, state_step, 0)
            lax.fori_loop(0, n_chunks, out_step, 0, unroll=2)
        s_ref[0, hh] = sf_run[hh]
        s_ref[1, hh] = sb_run[hh]


def _retention(pm3, rd4, ropes, s0):
    B, L, _ = pm3.shape
    n_chunks = L // CHUNK
    has_ctx = s0 is not None
    H = RET_HEADS
    HB = 1 if n_chunks > 2 else H
    kw, vw = HB * RET_DK, HB * RET_DV
    in_specs = [
        pl.BlockSpec((None, L, kw), lambda b, h: (b, 0, OFF_RQ // kw + h)),
        pl.BlockSpec((None, L, kw), lambda b, h: (b, 0, OFF_RK // kw + h)),
        pl.BlockSpec((None, L, vw), lambda b, h: (b, 0, OFF_RV // vw + h)),
        pl.BlockSpec((None, L, vw), lambda b, h: (b, 0, OFF_RG // vw + h)),
        pl.BlockSpec((2, HB, 1, RET_DV), lambda b, h: (0, h, 0, 0)),
    ]
    args = [pm3, pm3, pm3, pm3, rd4]
    if has_ctx:
        in_specs += [
            pl.BlockSpec((L, 128), lambda b, h: (0, 0)),
            pl.BlockSpec((L, 128), lambda b, h: (0, 0)),
            pl.BlockSpec((None, 2, HB, RET_DK, RET_DV), lambda b, h: (b, 0, h, 0, 0)),
        ]
        args += [ropes[0], ropes[1], s0]
    return pl.pallas_call(
        functools.partial(_ret_kernel, n_chunks=n_chunks, has_ctx=has_ctx, HB=HB),
        grid=(B, H // HB),
        in_specs=in_specs,
        out_specs=[
            pl.BlockSpec((None, L, vw), lambda b, h: (b, 0, h)),
            pl.BlockSpec((None, 2, HB, RET_DK, RET_DV), lambda b, h: (b, 0, h, 0, 0)),
        ],
        out_shape=[
            jax.ShapeDtypeStruct((B, L, D), F32),
            jax.ShapeDtypeStruct((B, 2, H, RET_DK, RET_DV), F32),
        ],
        scratch_shapes=[
            pltpu.VMEM((HB, L, RET_DK), F32),
            pltpu.VMEM((HB, L, RET_DK), F32),
            pltpu.VMEM((HB, n_chunks, 2 * RET_DK, RET_DV), BF16),
            pltpu.VMEM((HB, RET_DK, RET_DV), F32),
            pltpu.VMEM((HB, RET_DK, RET_DV), F32),
        ],
        compiler_params=_cparams(("arbitrary", "arbitrary")),
    )(*args)


def _lru_kernel(*refs, L, has_ctx):
    if has_ctx:
        (x_ref, gt_ref, cw_ref, cb_ref, w_ref, b_ref, lam_ref, h0_ref,
         o_ref, s_ref, xc_scr, hf_scr, pb_scr, ub_scr) = refs
    else:
        (x_ref, gt_ref, cw_ref, cb_ref, w_ref, b_ref, lam_ref,
         o_ref, s_ref, xc_scr, hf_scr, pb_scr, ub_scr) = refs
    G = LRU_GROUP
    NG = LRU_ITER_GROUPS
    n_it = L // (NG * G)
    W = LRU_BS
    BB = x_ref.shape[0]

    row = lax.broadcasted_iota(jnp.int32, (L, W), 0)
    for bb in range(BB):
        x = x_ref[bb]
        xm1 = jnp.where(row >= 1, pltpu.roll(x, 1, axis=0), 0.0)
        xp1 = jnp.where(row < L - 1, pltpu.roll(x, L - 1, axis=0), 0.0)
        xp2 = jnp.where(row < L - 2, pltpu.roll(x, L - 2, axis=0), 0.0)
        xc_scr[bb] = (cw_ref[0:1, :] * xm1 + cw_ref[1:2, :] * x + cw_ref[2:3, :] * xp1
                      + cw_ref[3:4, :] * xp2 + cb_ref[...])

    nsp_f = -LRU_C * _softplus(-lam_ref[0:1, :])
    nsp_b = -LRU_C * _softplus(-lam_ref[1:2, :])
    row8 = lax.broadcasted_iota(jnp.int32, (8, W), 0)

    def gates(zr, zi, nsp, xc):
        log_a = _sigmoid(zr) * nsp
        a = jnp.exp(log_a)
        u = jnp.sqrt(-jnp.tanh(log_a) * (a * a + 1.0)) * (_sigmoid(zi) * xc)
        return a, u

    def seg_scan(p, hl, carry, reverse):
        a, u = p, hl
        for s in (1, 2, 4):
            sh = (8 - s) if reverse else s
            a_sh = pltpu.roll(a, sh, axis=0)
            u_sh = pltpu.roll(u, sh, axis=0)
            m = (row8 < 8 - s) if reverse else (row8 >= s)
            u = jnp.where(m, a * u_sh + u, u)
            a = jnp.where(m, a * a_sh, a)
        h_end = u + a * carry
        if reverse:
            h_in = jnp.where(row8 == 7, carry, pltpu.roll(h_end, 7, axis=0))
            return h_in, h_end[0:1]
        h_in = jnp.where(row8 == 0, carry, pltpu.roll(h_end, 1, axis=0))
        return h_in, h_end[7:8]

    def pass1(it, carries):
        return tuple(pass1_one(it, carries[bb], xc_scr.at[bb], hf_scr.at[bb], pb_scr.at[bb], ub_scr.at[bb])
                     for bb in range(BB))

    def pass2(t, carries):
        return tuple(pass2_one(t, carries[bb], gt_ref.at[bb], o_ref.at[bb], hf_scr.at[bb], pb_scr.at[bb],
                               ub_scr.at[bb]) for bb in range(BB))

    def pass1_one(it, carry, xc_scr, hf_scr, pb_scr, ub_scr):
        base = pl.multiple_of(it * (NG * G), NG * G)
        xg = jnp.concatenate([xc_scr[pl.ds(base + gi * G + r, 8, stride=8), :]
                              for gi in range(NG) for r in range(8)], axis=0)
        zz = _dot(xg.astype(BF16), w_ref[...]) + b_ref[...]
        a_f, u_f = gates(zz[:, 0:W], zz[:, W:2 * W], nsp_f, xg)
        a_b, u_b = gates(zz[:, 2 * W:3 * W], zz[:, 3 * W:4 * W], nsp_b, xg)
        for gi in range(NG):
            v = lambda arr, r: arr[gi * G + r * 8:gi * G + (r + 1) * 8]
            p = [v(a_f, 0)]
            hl = [v(u_f, 0)]
            for r in range(1, 8):
                hl.append(v(a_f, r) * hl[-1] + v(u_f, r))
                p.append(v(a_f, r) * p[-1])
            h_in, carry = seg_scan(p[7], hl[7], carry, False)
            hf_scr[pl.ds(base + gi * G, G), :] = jnp.concatenate(
                [hl[r] + p[r] * h_in for r in range(8)], axis=0)
            pb = [None] * 8
            hb = [None] * 8
            pb[7] = v(a_b, 7)
            hb[7] = v(u_b, 7)
            for r in range(6, -1, -1):
                hb[r] = v(a_b, r) * hb[r + 1] + v(u_b, r)
                pb[r] = v(a_b, r) * pb[r + 1]
            pb_scr[pl.ds(base + gi * G, G), :] = jnp.concatenate(pb, axis=0)
            ub_scr[pl.ds(base + gi * G, G), :] = jnp.concatenate(hb, axis=0)
        return carry

    def pass2_one(t, carry, gt_ref, o_ref, hf_scr, pb_scr, ub_scr):
        base = pl.multiple_of((n_it - 1 - t) * (NG * G), NG * G)
        for gi in range(NG - 1, -1, -1):
            gb = base + gi * G
            pb = pb_scr[pl.ds(gb, G), :]
            hb = ub_scr[pl.ds(gb, G), :]
            hf = hf_scr[pl.ds(gb, G), :]
            h_in, carry = seg_scan(pb[0:8], hb[0:8], carry, True)
            for r in range(8):
                sl = slice(r * 8, (r + 1) * 8)
                gt = gt_ref[pl.ds(gb + r, 8, stride=8), :]
                o_ref[pl.ds(gb + r, 8, stride=8), :] = (hf[sl] + hb[sl] + pb[sl] * h_in) * _silu(gt)
        return carry

    if has_ctx:
        h0f = tuple(h0_ref[bb, 0:1, :] for bb in range(BB))
        h0b = tuple(h0_ref[bb, 1:2, :] for bb in range(BB))
    else:
        h0f = h0b = tuple(jnp.zeros((1, W), F32) for _ in range(BB))
    if n_it == 1:
        lf = pass1(0, h0f)
        lb = pass2(0, h0b)
    else:
        lf = lax.fori_loop(0, n_it, pass1, h0f)
        lb = lax.fori_loop(0, n_it, pass2, h0b)
    for bb in range(BB):
        s_ref[bb, 0:1, :] = lf[bb]
        s_ref[bb, 1:2, :] = lb[bb]


def _rglru(pm3, conv_w, conv_b, w_gate, b_gate, lam, h0):
    B, L, _ = pm3.shape
    has_ctx = h0 is not None
    W = LRU_BS
    BB = LRU_BATCH_PAIR
    in_specs = [
        pl.BlockSpec((BB, L, W), lambda b, n: (b, 0, OFF_LX // W + n)),
        pl.BlockSpec((BB, L, W), lambda b, n: (b, 0, OFF_LG // W + n)),
        pl.BlockSpec((4, W), lambda b, n: (0, n)),
        pl.BlockSpec((1, W), lambda b, n: (0, n)),
        pl.BlockSpec((None, W, 4 * W), lambda b, n: (n, 0, 0)),
        pl.BlockSpec((None, 1, 4 * W), lambda b, n: (n, 0, 0)),
        pl.BlockSpec((2, W), lambda b, n: (0, n)),
    ]
    args = [pm3, pm3, conv_w, conv_b, w_gate, b_gate, lam]
    if has_ctx:
        in_specs.append(pl.BlockSpec((BB, 2, W), lambda b, n: (b, 0, n)))
        args.append(h0)
    return pl.pallas_call(
        functools.partial(_lru_kernel, L=L, has_ctx=has_ctx),
        grid=(B // BB, LRU_BLOCKS),
        in_specs=in_specs,
        out_specs=[
            pl.BlockSpec((BB, L, W), lambda b, n: (b, 0, n)),
            pl.BlockSpec((BB, 2, W), lambda b, n: (b, 0, n)),
        ],
        out_shape=[
            jax.ShapeDtypeStruct((B, L, D), F32),
            jax.ShapeDtypeStruct((B, 2, D), F32),
        ],
        scratch_shapes=[pltpu.VMEM((BB, L, W), F32)] * 4,
        compiler_params=_cparams(("arbitrary", "arbitrary")),
    )(*args)


def _att_kernel(*refs, L, Q, has_ctx):
    if has_ctx:
        (sink_ref, q_ref, ag_ref, k_ref, v_ref, ck_ref, cv_ref, cosq_ref, sinq_ref, cosk_ref, sink_tab_ref,
         o_ref, km, vm, ckm, cvm) = refs
    else:
        (sink_ref, q_ref, ag_ref, k_ref, v_ref, o_ref, km, vm) = refs
    G = ATT_GROUP
    KW = ATT_KV_W
    qi = pl.program_id(1)

    def head_masked(dst, val):
        head = lax.shift_right_logical(lax.broadcasted_iota(jnp.int32, val.shape, 1), 6)
        for h in range(ATT_KV_HEADS):
            dst[h] = jnp.where(head == h, val, 0.0).astype(BF16)

    @pl.when(qi == 0)
    def _():
        kk = k_ref[...]
        if has_ctx:
            kk = _rope(kk, jnp.concatenate([cosk_ref[...]] * 2, axis=1),
                       jnp.concatenate([sink_tab_ref[...]] * 2, axis=1), 16)
            head_masked(ckm, ck_ref[...])
            head_masked(cvm, cv_ref[...])
        head_masked(km, kk)
        head_masked(vm, v_ref[...])

    q = q_ref[...] * (ATT_HD ** -0.5)
    if has_ctx:
        q = _rope(q, jnp.concatenate([cosq_ref[...]] * 8, axis=1),
                  jnp.concatenate([sinq_ref[...]] * 8, axis=1), 16)
        span = Q + 2 * WINDOW
        start = pl.multiple_of(jnp.clip((qi - 1) * Q, 0, L - span), Q)
        qpos = qi * Q + lax.broadcasted_iota(jnp.int32, (Q, span), 0)
        kpos = start + lax.broadcasted_iota(jnp.int32, (Q, span), 1)
        band = jnp.abs(qpos - kpos) <= WINDOW
        band4 = jnp.concatenate([band] * G, axis=0)
    qg = jnp.concatenate([q[:, g * KW:(g + 1) * KW] for g in range(G)], axis=0).astype(BF16)

    o = None
    for h in range(ATT_KV_HEADS):
        sk = jnp.concatenate(
            [jnp.full((Q, 1), sink_ref[h * G + g], F32) for g in range(G)], axis=0)
        if has_ctx:
            s = jnp.concatenate([jnp.where(band4, _dot_nt(qg, km[h, pl.ds(start, span), :]), -jnp.inf),
                                 _dot_nt(qg, ckm[h])], axis=1)
            m = jnp.maximum(jnp.max(s, axis=-1, keepdims=True), sk)
            p = jnp.exp(s - m)
            den = jnp.sum(p, axis=-1, keepdims=True) + jnp.exp(sk - m)
            pb = p.astype(BF16)
            oh = _dot(pb[:, :span], vm[h, pl.ds(start, span), :]) + _dot(pb[:, span:], cvm[h])
        else:
            s1 = _dot_nt(qg, km[h])
            m = jnp.maximum(jnp.max(s1, axis=-1, keepdims=True), sk)
            p1 = jnp.exp(s1 - m)
            den = jnp.sum(p1, axis=-1, keepdims=True) + jnp.exp(sk - m)
            oh = _dot(p1.astype(BF16), vm[h])
        oh = oh * (1.0 / den)
        o = oh if o is None else o + oh
    ao = jnp.concatenate([o[g * Q:(g + 1) * Q] for g in range(G)], axis=1)
    o_ref[...] = ao * _silu(ag_ref[...])


def _attention(pm3, sink, ropes, ck, cv):
    B, L, _ = pm3.shape
    has_ctx = ck is not None
    Q = CHUNK if has_ctx else L
    KW = ATT_KV_W
    in_specs = [
        pl.BlockSpec(memory_space=pltpu.SMEM),
        pl.BlockSpec((None, Q, D), lambda b, i: (b, i, OFF_AQ // D)),
        pl.BlockSpec((None, Q, D), lambda b, i: (b, i, OFF_AG // D)),
        pl.BlockSpec((None, L, KW), lambda b, i: (b, 0, OFF_AK // KW)),
        pl.BlockSpec((None, L, KW), lambda b, i: (b, 0, OFF_AV // KW)),
    ]
    args = [sink, pm3, pm3, pm3, pm3]
    scratch = [pltpu.VMEM((ATT_KV_HEADS, L, KW), BF16), pltpu.VMEM((ATT_KV_HEADS, L, KW), BF16)]
    if has_ctx:
        P = ck.shape[1]
        in_specs += [
            pl.BlockSpec((None, P, KW), lambda b, i: (b, 0, 0)),
            pl.BlockSpec((None, P, KW), lambda b, i: (b, 0, 0)),
            pl.BlockSpec((Q, 128), lambda b, i: (i, 0)),
            pl.BlockSpec((Q, 128), lambda b, i: (i, 0)),
            pl.BlockSpec((L, 128), lambda b, i: (0, 0)),
            pl.BlockSpec((L, 128), lambda b, i: (0, 0)),
        ]
        args += [ck, cv, ropes[0], ropes[1], ropes[0], ropes[1]]
        scratch += [pltpu.VMEM((ATT_KV_HEADS, P, KW), BF16), pltpu.VMEM((ATT_KV_HEADS, P, KW), BF16)]
    return pl.pallas_call(
        functools.partial(_att_kernel, L=L, Q=Q, has_ctx=has_ctx),
        grid=(B, L // Q),
        in_specs=in_specs,
        out_specs=pl.BlockSpec((None, Q, D), lambda b, i: (b, i, 0)),
        out_shape=jax.ShapeDtypeStruct((B, L, D), F32),
        scratch_shapes=scratch,
        compiler_params=_cparams(("arbitrary", "arbitrary")),
    )(*args)


OUT_TM = 512


def _out_kernel(x_ref, ro_ref, lo_ref, ao_ref, mod_ref, gpre_ref, gpost_ref,
                wm_ref, bm_ref, wb_ref, wo_ref, y_ref):
    x = x_ref[...]
    mod = mod_ref[...]
    h = _modulated_norm(x, gpre_ref[...], mod).astype(BF16)
    merged = None
    for n, br_ref in enumerate((ro_ref, lo_ref, ao_ref)):
        gate = _sigmoid(_dot(h, wm_ref[:, n * D:(n + 1) * D]) + bm_ref[:, n * D:(n + 1) * D])
        z = gate * _dot(br_ref[...].astype(BF16), wb_ref[n])
        merged = z if merged is None else merged + z
    out = _dot(merged.astype(BF16), wo_ref[...])
    nrm = out * lax.rsqrt(jnp.mean(out * out, axis=-1, keepdims=True) + EPS) * gpost_ref[...]
    y_ref[...] = x + mod[:, 2 * D:3 * D] * nrm


def _merge_out(x2d, ro, lo, ao, mod3, row_fn, g_pre, g_post, w_merge, b_merge, w_branch, w_out):
    T = x2d.shape[0]
    tm = OUT_TM
    tok = lambda i: (i, 0)
    once = pl.Buffered(1)
    return pl.pallas_call(
        _out_kernel,
        grid=(T // tm,),
        in_specs=[
            pl.BlockSpec((tm, D), tok),
            pl.BlockSpec((tm, D), tok),
            pl.BlockSpec((tm, D), tok),
            pl.BlockSpec((tm, D), tok),
            pl.BlockSpec((None, 1, 3 * D), lambda i: (row_fn(i * tm), 0, 0)),
            pl.BlockSpec((1, D), lambda i: (0, 0)),
            pl.BlockSpec((1, D), lambda i: (0, 0)),
            pl.BlockSpec((D, N_BRANCH * D), lambda i: (0, 0), pipeline_mode=once),
            pl.BlockSpec((1, N_BRANCH * D), lambda i: (0, 0)),
            pl.BlockSpec((N_BRANCH, D, D), lambda i: (0, 0, 0), pipeline_mode=once),
            pl.BlockSpec((D, D), lambda i: (0, 0), pipeline_mode=once),
        ],
        out_specs=pl.BlockSpec((tm, D), tok),
        out_shape=jax.ShapeDtypeStruct((T, D), F32),
        compiler_params=_cparams(("arbitrary",)),
    )(x2d, ro, lo, ao, mod3, g_pre, g_post, w_merge, b_merge, w_branch, w_out)


def _layer(x, mod3, row_of_token, lw, ctx, ropes_ret, ropes_att):
    B, L, _ = x.shape
    x2d = x.reshape(B * L, D)
    pm2 = _in_proj(x2d, mod3, row_of_token, lw['g_pre'], lw['w_in'])
    pm3 = pm2.reshape(B, L, W_IN)
    if ctx is None:
        ro, ret_s = _retention(pm3, lw['rd4'], None, None)
        lo, lru_s = _rglru(pm3, lw['conv_w'], lw['conv_b'], lw['w_gate'], lw['b_gate'], lw['lam'], None)
        ao = _attention(pm3, lw['sink'], None, None, None)
    else:
        s_ret, s_lru, ck, cv = ctx
        ro, ret_s = _retention(pm3, lw['rd4'], ropes_ret, s_ret)
        lo, lru_s = _rglru(pm3, lw['conv_w'], lw['conv_b'], lw['w_gate'], lw['b_gate'], lw['lam'], s_lru)
        ao = _attention(pm3, lw['sink'], ropes_att, ck, cv)
    y = _merge_out(x2d, ro.reshape(B * L, D), lo.reshape(B * L, D), ao.reshape(B * L, D),
                   mod3, row_of_token, lw['g_pre'], lw['g_post'], lw['w_merge'], lw['b_merge'],
                   lw['w_branch'], lw['w_out'])
    return y.reshape(B, L, D), pm3, ret_s, lru_s


def _group_major(w, axis):
    shp = w.shape
    w = w.reshape(shp[:axis] + (ATT_KV_HEADS, ATT_GROUP, ATT_HD) + shp[axis + 1:])
    w = jnp.swapaxes(w, axis, axis + 1)
    return w.reshape(shp)


def _layer_weights(l, norm_pre, norm_post, w_in, ret_decay, conv_w, conv_b, lru_wa, lru_ba, lru_wx,
                   lru_bx, lru_lambda, att_sink, w_branch, w_merge, b_merge, w_out):
    offs = np.cumsum((512, 512, 1024, 1024, 1024, 1024, 1024, 256, 256, 1024))[:-1].tolist()
    rq, rk, rv, rg, lx, lgt, aq, ak, av, ag = jnp.split(w_in[l], offs, axis=-1)
    w_cat = jnp.concatenate([_group_major(aq, 1), _group_major(ag, 1), rv, rg, lx, lgt, rq, rk, ak, av],
                            axis=-1).astype(BF16)
    wb = jnp.stack([w_branch[l, 0], w_branch[l, 1], _group_major(w_branch[l, 2], 0)], axis=0).astype(BF16)
    w_gate = jnp.concatenate([lru_wa[l, 0], lru_wx[l, 0], lru_wa[l, 1], lru_wx[l, 1]], axis=-1).astype(BF16)
    b4 = jnp.stack([lru_ba[l, 0], lru_bx[l, 0], lru_ba[l, 1], lru_bx[l, 1]], axis=0)
    b_gate = b4.reshape(4, LRU_BLOCKS, LRU_BS).transpose(1, 0, 2).reshape(LRU_BLOCKS, 1, 4 * LRU_BS)
    rd4 = jnp.broadcast_to(ret_decay[l][:, :, None, None], (2, RET_HEADS, 1, RET_DV))
    return {
        'g_pre': norm_pre[l].reshape(1, D), 'g_post': norm_post[l].reshape(1, D),
        'w_in': w_cat, 'rd4': rd4,
        'conv_w': conv_w[l], 'conv_b': conv_b[l].reshape(1, D),
        'w_gate': w_gate, 'b_gate': b_gate, 'lam': lru_lambda[l],
        'sink': att_sink[l], 'w_merge': w_merge[l].astype(BF16), 'b_merge': b_merge[l].reshape(1, N_BRANCH * D),
        'w_branch': wb, 'w_out': w_out[l].astype(BF16),
    }


def kernel(x_prompt, x_sample, cache_k, cache_v, state_ret, state_lru, c, c_ctx, w_ada, b_ada, norm_pre, norm_post, w_in, ret_decay, conv_w, conv_b, lru_wa, lru_ba, lru_wx, lru_bx, lru_lambda, att_sink, w_branch, w_merge, b_merge, w_out):
    B, S, _ = x_prompt.shape
    Bd, Ld, _ = x_sample.shape
    P = cache_k.shape[2]
    cond8 = jnp.concatenate([c_ctx[None, :], c, jnp.zeros((8 - 1 - Bd, D), F32)], axis=0)
    mod = _ada_mod(cond8, w_ada, b_ada)
    ropes_ret = _rope_tables(Ld, RET_DK)
    ropes_att = _rope_tables(Ld, ATT_HD)

    yp, ys = x_prompt, x_sample
    new_k, new_v, new_ret, new_lru = [], [], [], []
    for l in range(DEPTH):
        lw = _layer_weights(l, norm_pre, norm_post, w_in, ret_decay, conv_w, conv_b, lru_wa, lru_ba,
                            lru_wx, lru_bx, lru_lambda, att_sink, w_branch, w_merge, b_merge, w_out)
        mod3 = mod[l].reshape(8, 1, 3 * D)
        yp, pm_p, r_s, l_s = _layer(yp, mod3, lambda t: 0, lw, None, None, None)
        new_ret.append(r_s)
        new_lru.append(l_s)
        new_k.append(pm_p[:, :, OFF_AK:OFF_AK + ATT_KV_W].reshape(B, S, ATT_KV_HEADS, ATT_HD))
        new_v.append(pm_p[:, :, OFF_AV:OFF_AV + ATT_KV_W].reshape(B, S, ATT_KV_HEADS, ATT_HD))
        ctx = (state_ret[:, l], state_lru[:, l],
               cache_k[:, l].reshape(Bd, P, ATT_KV_W), cache_v[:, l].reshape(Bd, P, ATT_KV_W))
        ys, _, _, _ = _layer(ys, mod3, lambda t: 1 + t // Ld, lw, ctx, ropes_ret, ropes_att)
    return (yp, ys, jnp.stack(new_k, axis=1), jnp.stack(new_v, axis=1),
            jnp.stack(new_ret, axis=1), jnp.stack(new_lru, axis=1))
```

```python
import functools

import numpy as np
import jax
import jax.numpy as jnp
from jax import lax
from jax.experimental import pallas as pl
from jax.experimental.pallas import tpu as pltpu

F32 = jnp.float32
BF16 = jnp.bfloat16

D = 1024
DEPTH = 2
GRID_W = 64
EPS = 1e-6
N_BRANCH = 3
RET_HEADS = 4
RET_DV = 256
RET_DK = 128
CHUNK = 128
LRU_BLOCKS = 8
LRU_BS = 128
LRU_C = 8.0
LRU_GROUP = 64
LRU_ITER_GROUPS = 4
LRU_BATCH_PAIR = 2
ATT_HD = 64
ATT_Q_HEADS = 16
ATT_KV_HEADS = 4
ATT_GROUP = 4
ATT_KV_W = 256
WINDOW = 128
ROPE_BASE = 10000.0

OFF_AQ = 0
OFF_AG = 1024
OFF_RV = 2048
OFF_RG = 3072
OFF_LX = 4096
OFF_LG = 5120
OFF_RQ = 6144
OFF_RK = 6656
OFF_AK = 7168
OFF_AV = 7424
W_IN = 7680

VMEM_LIMIT = 56 * 1024 * 1024


def _cparams(sem):
    return pltpu.CompilerParams(dimension_semantics=sem, vmem_limit_bytes=VMEM_LIMIT)


def _sigmoid(x):
    return 0.5 * jnp.tanh(0.5 * x) + 0.5


def _silu(x):
    return x * _sigmoid(x)


def _softplus(z):
    return jnp.maximum(z, 0.0) + jnp.log1p(jnp.exp(-jnp.abs(z)))


def _dot(a, b):
    return jnp.dot(a, b, preferred_element_type=F32)


def _dot_nt(a, b):
    return lax.dot_general(a, b, (((1,), (1,)), ((), ())), preferred_element_type=F32)


def _dot_tn(a, b):
    return lax.dot_general(a, b, (((0,), (0,)), ((), ())), preferred_element_type=F32)


def _modulated_norm(x, g, mod):
    y = x * lax.rsqrt(jnp.mean(x * x, axis=-1, keepdims=True) + EPS) * g
    return y * (1.0 + mod[:, D:2 * D]) + mod[:, 0:D]


def _ada_kernel(c_ref, w_ref, b_ref, o_ref):
    s = _silu(c_ref[...])
    o_ref[...] = _dot(s.astype(BF16), w_ref[...].astype(BF16)) + b_ref[...]


def _ada_mod(cond8, w_ada, b_ada):
    tn = 1024
    return pl.pallas_call(
        _ada_kernel,
        grid=(DEPTH, 3 * D // tn),
        in_specs=[
            pl.BlockSpec((8, D), lambda l, j: (0, 0)),
            pl.BlockSpec((None, D, tn), lambda l, j: (l, 0, j)),
            pl.BlockSpec((None, 1, tn), lambda l, j: (l, 0, j)),
        ],
        out_specs=pl.BlockSpec((None, 8, tn), lambda l, j: (l, 0, j)),
        out_shape=jax.ShapeDtypeStruct((DEPTH, 8, 3 * D), F32),
        compiler_params=_cparams(("arbitrary", "arbitrary")),
    )(cond8, w_ada, b_ada.reshape(DEPTH, 1, 3 * D))


IN_TM = 1024
IN_TN = 1280


def _in_kernel(x_ref, mod_ref, g_ref, w_ref, o_ref, h_all):
    j = pl.program_id(0)
    i = pl.program_id(1)
    r0 = pl.multiple_of(i * IN_TM, IN_TM)

    @pl.when(j == 0)
    def _():
        h = _modulated_norm(x_ref[...], g_ref[...], mod_ref[...])
        h_all[pl.ds(r0, IN_TM), :] = h.astype(BF16)

    o_ref[...] = _dot(h_all[pl.ds(r0, IN_TM), :], w_ref[...])


def _in_proj(x2d, mod3, row_fn, g_pre, w_in):
    T = x2d.shape[0]
    n_m = T // IN_TM
    return pl.pallas_call(
        _in_kernel,
        grid=(W_IN // IN_TN, n_m),
        in_specs=[
            pl.BlockSpec((IN_TM, D), lambda j, i: (jnp.where(j == 0, i, n_m - 1), 0)),
            pl.BlockSpec((None, 1, 3 * D), lambda j, i: (row_fn(i * IN_TM), 0, 0)),
            pl.BlockSpec((1, D), lambda j, i: (0, 0)),
            pl.BlockSpec((D, IN_TN), lambda j, i: (0, j)),
        ],
        out_specs=pl.BlockSpec((IN_TM, IN_TN), lambda j, i: (i, j)),
        out_shape=jax.ShapeDtypeStruct((T, W_IN), F32),
        scratch_shapes=[pltpu.VMEM((T, D), BF16)],
        compiler_params=_cparams(("arbitrary", "arbitrary")),
    )(x2d, mod3, g_pre, w_in)


def _rope(x, cos_t, sin_t, half):
    lane = lax.broadcasted_iota(jnp.int32, x.shape, 1)
    up = pltpu.roll(x, x.shape[1] - half, axis=1)
    dn = pltpu.roll(x, half, axis=1)
    partner = jnp.where((lane & half) == 0, up, dn)
    return x * cos_t + partner * sin_t


def _rope_tables(n_tokens, dim):
    nq = dim // 4
    lane = np.arange(128)
    within = lane % dim
    axis = within // (2 * nq)
    freq = within % nq
    sign = np.where((within % (2 * nq)) < nq, -1.0, 1.0).astype(np.float32)
    inv = ROPE_BASE ** (-jnp.arange(nq, dtype=F32) / nq)
    t = jnp.arange(n_tokens)
    pos = jnp.stack([(t // GRID_W).astype(F32), (t % GRID_W).astype(F32)], axis=1)
    ang = pos[:, axis] * inv[freq][None, :]
    return jnp.cos(ang), jnp.sin(ang) * sign[None, :]


def _ret_kernel(*refs, n_chunks, has_ctx, HB):
    if has_ctx:
        (q_ref, k_ref, v_ref, g_ref, rd_ref, cos_ref, sin_ref, s0_ref,
         o_ref, s_ref, qs, ks, s_all, sf_run, sb_run) = refs
    else:
        (q_ref, k_ref, v_ref, g_ref, rd_ref,
         o_ref, s_ref, qs, ks, s_all, sf_run, sb_run) = refs
    C = CHUNK
    ii = lax.broadcasted_iota(jnp.int32, (C, C), 0)
    jj = lax.broadcasted_iota(jnp.int32, (C, C), 1)
    diff = (ii - jj).astype(F32)
    ri = ii.astype(F32)

    steps = []
    for hh in range(HB):
        ksl = slice(hh * RET_DK, (hh + 1) * RET_DK)
        vsl = slice(hh * RET_DV, (hh + 1) * RET_DV)

        q = q_ref[:, ksl] * (RET_DK ** -0.5)
        k = k_ref[:, ksl]
        if has_ctx:
            q = _rope(q, cos_ref[...], sin_ref[...], 32)
            k = _rope(k, cos_ref[...], sin_ref[...], 32)
            sf_run[hh] = s0_ref[0, hh]
            sb_run[hh] = s0_ref[1, hh]
        else:
            sf_run[hh] = jnp.zeros((RET_DK, RET_DV), F32)
            sb_run[hh] = jnp.zeros((RET_DK, RET_DV), F32)
        qs[hh] = q
        ks[hh] = k

        lg_f = -_softplus(-rd_ref[0, hh])
        lg_b = -_softplus(-rd_ref[1, hh])
        lgf = lg_f[:, :C]
        lgb = lg_b[:, :C]
        dmat = (jnp.where(diff >= 0, jnp.exp(lgf * jnp.maximum(diff, 0.0)), 0.0)
                + jnp.where(diff <= 0, jnp.exp(lgb * jnp.maximum(-diff, 0.0)), 0.0))
        dq_f = jnp.exp(lgf * (ri + 1.0))
        dk_f = jnp.exp(lgf * (C - 1.0 - ri))
        dq_b = jnp.exp(lgb * (C - ri))
        dk_b = jnp.exp(lgb * ri)
        dc_f = jnp.exp(lg_f * float(C))
        dc_b = jnp.exp(lg_b * float(C))

        def state_step(t, carry, hh=hh, vsl=vsl, dk_f=dk_f, dk_b=dk_b, dc_f=dc_f, dc_b=dc_b):
            cf = t
            cb = n_chunks - 1 - t
            rf = pl.multiple_of(cf * C, C)
            rb = pl.multiple_of(cb * C, C)
            sf = sf_run[hh]
            sb = sb_run[hh]
            s_all[hh, cf, 0:RET_DK, :] = sf.astype(BF16)
            s_all[hh, cb, RET_DK:2 * RET_DK, :] = sb.astype(BF16)
            kf = (ks[hh, pl.ds(rf, C), :] * dk_f).astype(BF16)
            kb = (ks[hh, pl.ds(rb, C), :] * dk_b).astype(BF16)
            sf_run[hh] = sf * dc_f + _dot_tn(kf, v_ref[pl.ds(rf, C), vsl].astype(BF16))
            sb_run[hh] = sb * dc_b + _dot_tn(kb, v_ref[pl.ds(rb, C), vsl].astype(BF16))
            return carry

        def out_step(c, carry, hh=hh, vsl=vsl, dmat=dmat, dq_f=dq_f, dq_b=dq_b):
            r0 = pl.multiple_of(c * C, C)
            qc = qs[hh, pl.ds(r0, C), :]
            kc = ks[hh, pl.ds(r0, C), :]
            vc = v_ref[pl.ds(r0, C), vsl].astype(BF16)
            a = _dot_nt(qc.astype(BF16), kc.astype(BF16)) * dmat
            qq = jnp.concatenate([qc * dq_f, qc * dq_b], axis=1).astype(BF16)
            o = _dot(a.astype(BF16), vc) + _dot(qq, s_all[hh, c])
            mu = jnp.mean(o, axis=-1, keepdims=True)
            oc = o - mu
            on = oc * lax.rsqrt(jnp.mean(oc * oc, axis=-1, keepdims=True) + EPS)
            o_ref[pl.ds(r0, C), vsl] = on * _silu(g_ref[pl.ds(r0, C), vsl])
            return carry

        steps.append((state_step, out_step))

    def all_state(t, carry):
        for state_step, _ in steps:
            state_step(t, 0)
        return carry

    def all_out(c, carry):
        for _, out_step in steps:
            out_step(c, 0)
        return carry

    if n_chunks <= 2:
        for t in range(n_chunks):
            all_state(t, 0)
        for c in range(n_chunks):
            all_out(c, 0)
    else:
        lax.fori_loop(0, n_chunks, all_state, 0)
        lax.fori_loop(0, n_chunks, all_out, 0, unroll=2)
    for hh in range(HB):
        s_ref[0, hh] = sf_run[hh]
        s_ref[1, hh] = sb_run[hh]


def _retention(pm3, rd4, ropes, s0):
    B, L, _ = pm3.shape
    n_chunks = L // CHUNK
    has_ctx = s0 is not None
    H = RET_HEADS
    HB = 2 if n_chunks > 2 else H
    kw, vw = HB * RET_DK, HB * RET_DV
    in_specs = [
        pl.BlockSpec((None, L, kw), lambda b, h: (b, 0, OFF_RQ // kw + h)),
        pl.BlockSpec((None, L, kw), lambda b, h: (b, 0, OFF_RK // kw + h)),
        pl.BlockSpec((None, L, vw), lambda b, h: (b, 0, OFF_RV // vw + h)),
        pl.BlockSpec((None, L, vw), lambda b, h: (b, 0, OFF_RG // vw + h)),
        pl.BlockSpec((2, HB, 1, RET_DV), lambda b, h: (0, h, 0, 0)),
    ]
    args = [pm3, pm3, pm3, pm3, rd4]
    if has_ctx:
        in_specs += [
            pl.BlockSpec((L, 128), lambda b, h: (0, 0)),
            pl.BlockSpec((L, 128), lambda b, h: (0, 0)),
            pl.BlockSpec((None, 2, HB, RET_DK, RET_DV), lambda b, h: (b, 0, h, 0, 0)),
        ]
        args += [ropes[0], ropes[1], s0]
    return pl.pallas_call(
        functools.partial(_ret_kernel, n_chunks=n_chunks, has_ctx=has_ctx, HB=HB),
        grid=(B, H // HB),
        in_specs=in_specs,
        out_specs=[
            pl.BlockSpec((None, L, vw), lambda b, h: (b, 0, h)),
            pl.BlockSpec((None, 2, HB, RET_DK, RET_DV), lambda b, h: (b, 0, h, 0, 0)),
        ],
        out_shape=[
            jax.ShapeDtypeStruct((B, L, D), F32),
            jax.ShapeDtypeStruct((B, 2, H, RET_DK, RET_DV), F32),
        ],
        scratch_shapes=[
            pltpu.VMEM((HB, L, RET_DK), F32),
            pltpu.VMEM((HB, L, RET_DK), F32),
            pltpu.VMEM((HB, n_chunks, 2 * RET_DK, RET_DV), BF16),
            pltpu.VMEM((HB, RET_DK, RET_DV), F32),
            pltpu.VMEM((HB, RET_DK, RET_DV), F32),
        ],
        compiler_params=_cparams(("arbitrary", "arbitrary")),
    )(*args)


def _lru_kernel(*refs, L, has_ctx):
    if has_ctx:
        (x_ref, gt_ref, cw_ref, cb_ref, w_ref, b_ref, lam_ref, h0_ref,
         o_ref, s_ref, xc_scr, hf_scr, pb_scr, ub_scr) = refs
    else:
        (x_ref, gt_ref, cw_ref, cb_ref, w_ref, b_ref, lam_ref,
         o_ref, s_ref, xc_scr, hf_scr, pb_scr, ub_scr) = refs
    G = LRU_GROUP
    NG = LRU_ITER_GROUPS
    n_it = L // (NG * G)
    W = LRU_BS
    BB = x_ref.shape[0]

    row = lax.broadcasted_iota(jnp.int32, (L, W), 0)
    for bb in range(BB):
        x = x_ref[bb]
        xm1 = jnp.where(row >= 1, pltpu.roll(x, 1, axis=0), 0.0)
        xp1 = jnp.where(row < L - 1, pltpu.roll(x, L - 1, axis=0), 0.0)
        xp2 = jnp.where(row < L - 2, pltpu.roll(x, L - 2, axis=0), 0.0)
        xc_scr[bb] = (cw_ref[0:1, :] * xm1 + cw_ref[1:2, :] * x + cw_ref[2:3, :] * xp1
                      + cw_ref[3:4, :] * xp2 + cb_ref[...])

    nsp_f = -LRU_C * _softplus(-lam_ref[0:1, :])
    nsp_b = -LRU_C * _softplus(-lam_ref[1:2, :])
    row8 = lax.broadcasted_iota(jnp.int32, (8, W), 0)

    def gates(zr, zi, nsp, xc):
        log_a = _sigmoid(zr) * nsp
        a = jnp.exp(log_a)
        u = jnp.sqrt(-jnp.tanh(log_a) * (a * a + 1.0)) * (_sigmoid(zi) * xc)
        return a, u

    def seg_scan(p, hl, carry, reverse):
        a, u = p, hl
        for s in (1, 2, 4):
            sh = (8 - s) if reverse else s
            a_sh = pltpu.roll(a, sh, axis=0)
            u_sh = pltpu.roll(u, sh, axis=0)
            m = (row8 < 8 - s) if reverse else (row8 >= s)
            u = jnp.where(m, a * u_sh + u, u)
            a = jnp.where(m, a * a_sh, a)
        h_end = u + a * carry
        if reverse:
            h_in = jnp.where(row8 == 7, carry, pltpu.roll(h_end, 7, axis=0))
            return h_in, h_end[0:1]
        h_in = jnp.where(row8 == 0, carry, pltpu.roll(h_end, 1, axis=0))
        return h_in, h_end[7:8]

    def pass1(it, carries):
        return tuple(pass1_one(it, carries[bb], xc_scr.at[bb], hf_scr.at[bb], pb_scr.at[bb], ub_scr.at[bb])
                     for bb in range(BB))

    def pass2(t, carries):
        return tuple(pass2_one(t, carries[bb], gt_ref.at[bb], o_ref.at[bb], hf_scr.at[bb], pb_scr.at[bb],
                               ub_scr.at[bb]) for bb in range(BB))

    def pass1_one(it, carry, xc_scr, hf_scr, pb_scr, ub_scr):
        base = pl.multiple_of(it * (NG * G), NG * G)
        xg = jnp.concatenate([xc_scr[pl.ds(base + gi * G + r, 8, stride=8), :]
                              for gi in range(NG) for r in range(8)], axis=0)
        zz = _dot(xg.astype(BF16), w_ref[...]) + b_ref[...]
        a_f, u_f = gates(zz[:, 0:W], zz[:, W:2 * W], nsp_f, xg)
        a_b, u_b = gates(zz[:, 2 * W:3 * W], zz[:, 3 * W:4 * W], nsp_b, xg)
        for gi in range(NG):
            v = lambda arr, r: arr[gi * G + r * 8:gi * G + (r + 1) * 8]
            p = [v(a_f, 0)]
            hl = [v(u_f, 0)]
            for r in range(1, 8):
                hl.append(v(a_f, r) * hl[-1] + v(u_f, r))
                p.append(v(a_f, r) * p[-1])
            h_in, carry = seg_scan(p[7], hl[7], carry, False)
            hf_scr[pl.ds(base + gi * G, G), :] = jnp.concatenate(
                [hl[r] + p[r] * h_in for r in range(8)], axis=0)
            pb = [None] * 8
            hb = [None] * 8
            pb[7] = v(a_b, 7)
            hb[7] = v(u_b, 7)
            for r in range(6, -1, -1):
                hb[r] = v(a_b, r) * hb[r + 1] + v(u_b, r)
                pb[r] = v(a_b, r) * pb[r + 1]
            pb_scr[pl.ds(base + gi * G, G), :] = jnp.concatenate(pb, axis=0)
            ub_scr[pl.ds(base + gi * G, G), :] = jnp.concatenate(hb, axis=0)
        return carry

    def pass2_one(t, carry, gt_ref, o_ref, hf_scr, pb_scr, ub_scr):
        base = pl.multiple_of((n_it - 1 - t) * (NG * G), NG * G)
        for gi in range(NG - 1, -1, -1):
            gb = base + gi * G
            pb = pb_scr[pl.ds(gb, G), :]
            hb = ub_scr[pl.ds(gb, G), :]
            hf = hf_scr[pl.ds(gb, G), :]
            h_in, carry = seg_scan(pb[0:8], hb[0:8], carry, True)
            for r in range(8):
                sl = slice(r * 8, (r + 1) * 8)
                gt = gt_ref[pl.ds(gb + r, 8, stride=8), :]
                o_ref[pl.ds(gb + r, 8, stride=8), :] = (hf[sl] + hb[sl] + pb[sl] * h_in) * _silu(gt)
        return carry

    if has_ctx:
        h0f = tuple(h0_ref[bb, 0:1, :] for bb in range(BB))
        h0b = tuple(h0_ref[bb, 1:2, :] for bb in range(BB))
    else:
        h0f = h0b = tuple(jnp.zeros((1, W), F32) for _ in range(BB))
    if n_it == 1:
        lf = pass1(0, h0f)
        lb = pass2(0, h0b)
    else:
        lf = lax.fori_loop(0, n_it, pass1, h0f)
        lb = lax.fori_loop(0, n_it, pass2, h0b)
    for bb in range(BB):
        s_ref[bb, 0:1, :] = lf[bb]
        s_ref[bb, 1:2, :] = lb[bb]


def _rglru(pm3, conv_w, conv_b, w_gate, b_gate, lam, h0):
    B, L, _ = pm3.shape
    has_ctx = h0 is not None
    W = LRU_BS
    BB = LRU_BATCH_PAIR
    in_specs = [
        pl.BlockSpec((BB, L, W), lambda b, n: (b, 0, OFF_LX // W + n)),
        pl.BlockSpec((BB, L, W), lambda b, n: (b, 0, OFF_LG // W + n)),
        pl.BlockSpec((4, W), lambda b, n: (0, n)),
        pl.BlockSpec((1, W), lambda b, n: (0, n)),
        pl.BlockSpec((None, W, 4 * W), lambda b, n: (n, 0, 0)),
        pl.BlockSpec((None, 1, 4 * W), lambda b, n: (n, 0, 0)),
        pl.BlockSpec((2, W), lambda b, n: (0, n)),
    ]
    args = [pm3, pm3, conv_w, conv_b, w_gate, b_gate, lam]
    if has_ctx:
        in_specs.append(pl.BlockSpec((BB, 2, W), lambda b, n: (b, 0, n)))
        args.append(h0)
    return pl.pallas_call(
        functools.partial(_lru_kernel, L=L, has_ctx=has_ctx),
        grid=(B // BB, LRU_BLOCKS),
        in_specs=in_specs,
        out_specs=[
            pl.BlockSpec((BB, L, W), lambda b, n: (b, 0, n)),
            pl.BlockSpec((BB, 2, W), lambda b, n: (b, 0, n)),
        ],
        out_shape=[
            jax.ShapeDtypeStruct((B, L, D), F32),
            jax.ShapeDtypeStruct((B, 2, D), F32),
        ],
        scratch_shapes=[pltpu.VMEM((BB, L, W), F32)] * 4,
        compiler_params=_cparams(("arbitrary", "arbitrary")),
    )(*args)


def _att_kernel(*refs, L, Q, has_ctx):
    if has_ctx:
        (sink_ref, q_ref, ag_ref, k_ref, v_ref, ck_ref, cv_ref, cosq_ref, sinq_ref, cosk_ref, sink_tab_ref,
         o_ref, km, vm, ckm, cvm) = refs
    else:
        (sink_ref, q_ref, ag_ref, k_ref, v_ref, o_ref, km, vm) = refs
    G = ATT_GROUP
    KW = ATT_KV_W
    qi = pl.program_id(1)

    def head_masked(dst, val):
        head = lax.shift_right_logical(lax.broadcasted_iota(jnp.int32, val.shape, 1), 6)
        for h in range(ATT_KV_HEADS):
            dst[h] = jnp.where(head == h, val, 0.0).astype(BF16)

    @pl.when(qi == 0)
    def _():
        kk = k_ref[...]
        if has_ctx:
            kk = _rope(kk, jnp.concatenate([cosk_ref[...]] * 2, axis=1),
                       jnp.concatenate([sink_tab_ref[...]] * 2, axis=1), 16)
            head_masked(ckm, ck_ref[...])
            head_masked(cvm, cv_ref[...])
        head_masked(km, kk)
        head_masked(vm, v_ref[...])

    q = q_ref[...] * (ATT_HD ** -0.5)
    if has_ctx:
        q = _rope(q, jnp.concatenate([cosq_ref[...]] * 8, axis=1),
                  jnp.concatenate([sinq_ref[...]] * 8, axis=1), 16)
        span = Q + 2 * WINDOW
        start = pl.multiple_of(jnp.clip((qi - 1) * Q, 0, L - span), Q)
        qpos = qi * Q + lax.broadcasted_iota(jnp.int32, (Q, span), 0)
        kpos = start + lax.broadcasted_iota(jnp.int32, (Q, span), 1)
        band = jnp.abs(qpos - kpos) <= WINDOW
        band4 = jnp.concatenate([band] * G, axis=0)
    qg = jnp.concatenate([q[:, g * KW:(g + 1) * KW] for g in range(G)], axis=0).astype(BF16)

    o = None
    for h in range(ATT_KV_HEADS):
        sk = jnp.concatenate(
            [jnp.full((Q, 1), sink_ref[h * G + g], F32) for g in range(G)], axis=0)
        if has_ctx:
            s = jnp.concatenate([jnp.where(band4, _dot_nt(qg, km[h, pl.ds(start, span), :]), -jnp.inf),
                                 _dot_nt(qg, ckm[h])], axis=1)
            m = jnp.maximum(jnp.max(s, axis=-1, keepdims=True), sk)
            p = jnp.exp(s - m)
            den = jnp.sum(p, axis=-1, keepdims=True) + jnp.exp(sk - m)
            pb = p.astype(BF16)
            oh = _dot(pb[:, :span], vm[h, pl.ds(start, span), :]) + _dot(pb[:, span:], cvm[h])
        else:
            s1 = _dot_nt(qg, km[h])
            m = jnp.maximum(jnp.max(s1, axis=-1, keepdims=True), sk)
            p1 = jnp.exp(s1 - m)
            den = jnp.sum(p1, axis=-1, keepdims=True) + jnp.exp(sk - m)
            oh = _dot(p1.astype(BF16), vm[h])
        oh = oh * (1.0 / den)
        o = oh if o is None else o + oh
    ao = jnp.concatenate([o[g * Q:(g + 1) * Q] for g in range(G)], axis=1)
    o_ref[...] = ao * _silu(ag_ref[...])


def _attention(pm3, sink, ropes, ck, cv):
    B, L, _ = pm3.shape
    has_ctx = ck is not None
    Q = CHUNK if has_ctx else L
    KW = ATT_KV_W
    in_specs = [
        pl.BlockSpec(memory_space=pltpu.SMEM),
        pl.BlockSpec((None, Q, D), lambda b, i: (b, i, OFF_AQ // D)),
        pl.BlockSpec((None, Q, D), lambda b, i: (b, i, OFF_AG // D)),
        pl.BlockSpec((None, L, KW), lambda b, i: (b, 0, OFF_AK // KW)),
        pl.BlockSpec((None, L, KW), lambda b, i: (b, 0, OFF_AV // KW)),
    ]
    args = [sink, pm3, pm3, pm3, pm3]
    scratch = [pltpu.VMEM((ATT_KV_HEADS, L, KW), BF16), pltpu.VMEM((ATT_KV_HEADS, L, KW), BF16)]
    if has_ctx:
        P = ck.shape[1]
        in_specs += [
            pl.BlockSpec((None, P, KW), lambda b, i: (b, 0, 0)),
            pl.BlockSpec((None, P, KW), lambda b, i: (b, 0, 0)),
            pl.BlockSpec((Q, 128), lambda b, i: (i, 0)),
            pl.BlockSpec((Q, 128), lambda b, i: (i, 0)),
            pl.BlockSpec((L, 128), lambda b, i: (0, 0)),
            pl.BlockSpec((L, 128), lambda b, i: (0, 0)),
        ]
        args += [ck, cv, ropes[0], ropes[1], ropes[0], ropes[1]]
        scratch += [pltpu.VMEM((ATT_KV_HEADS, P, KW), BF16), pltpu.VMEM((ATT_KV_HEADS, P, KW), BF16)]
    return pl.pallas_call(
        functools.partial(_att_kernel, L=L, Q=Q, has_ctx=has_ctx),
        grid=(B, L // Q),
        in_specs=in_specs,
        out_specs=pl.BlockSpec((None, Q, D), lambda b, i: (b, i, 0)),
        out_shape=jax.ShapeDtypeStruct((B, L, D), F32),
        scratch_shapes=scratch,
        compiler_params=_cparams(("arbitrary", "arbitrary")),
    )(*args)


OUT_TM = 512


def _out_kernel(x_ref, ro_ref, lo_ref, ao_ref, mod_ref, gpre_ref, gpost_ref,
                wm_ref, bm_ref, wb_ref, wo_ref, y_ref):
    x = x_ref[...]
    mod = mod_ref[...]
    h = _modulated_norm(x, gpre_ref[...], mod).astype(BF16)
    merged = None
    for n, br_ref in enumerate((ro_ref, lo_ref, ao_ref)):
        gate = _sigmoid(_dot(h, wm_ref[:, n * D:(n + 1) * D]) + bm_ref[:, n * D:(n + 1) * D])
        z = gate * _dot(br_ref[...].astype(BF16), wb_ref[n])
        merged = z if merged is None else merged + z
    out = _dot(merged.astype(BF16), wo_ref[...])
    nrm = out * lax.rsqrt(jnp.mean(out * out, axis=-1, keepdims=True) + EPS) * gpost_ref[...]
    y_ref[...] = x + mod[:, 2 * D:3 * D] * nrm


def _merge_out(x2d, ro, lo, ao, mod3, row_fn, g_pre, g_post, w_merge, b_merge, w_branch, w_out):
    T = x2d.shape[0]
    tm = OUT_TM
    tok = lambda i: (i, 0)
    once = pl.Buffered(1)
    return pl.pallas_call(
        _out_kernel,
        grid=(T // tm,),
        in_specs=[
            pl.BlockSpec((tm, D), tok),
            pl.BlockSpec((tm, D), tok),
            pl.BlockSpec((tm, D), tok),
            pl.BlockSpec((tm, D), tok),
            pl.BlockSpec((None, 1, 3 * D), lambda i: (row_fn(i * tm), 0, 0)),
            pl.BlockSpec((1, D), lambda i: (0, 0)),
            pl.BlockSpec((1, D), lambda i: (0, 0)),
            pl.BlockSpec((D, N_BRANCH * D), lambda i: (0, 0), pipeline_mode=once),
            pl.BlockSpec((1, N_BRANCH * D), lambda i: (0, 0)),
            pl.BlockSpec((N_BRANCH, D, D), lambda i: (0, 0, 0), pipeline_mode=once),
            pl.BlockSpec((D, D), lambda i: (0, 0), pipeline_mode=once),
        ],
        out_specs=pl.BlockSpec((tm, D), tok),
        out_shape=jax.ShapeDtypeStruct((T, D), F32),
        compiler_params=_cparams(("arbitrary",)),
    )(x2d, ro, lo, ao, mod3, g_pre, g_post, w_merge, b_merge, w_branch, w_out)


def _layer(x, mod3, row_of_token, lw, ctx, ropes_ret, ropes_att):
    B, L, _ = x.shape
    x2d = x.reshape(B * L, D)
    pm2 = _in_proj(x2d, mod3, row_of_token, lw['g_pre'], lw['w_in'])
    pm3 = pm2.reshape(B, L, W_IN)
    if ctx is None:
        ro, ret_s = _retention(pm3, lw['rd4'], None, None)
        lo, lru_s = _rglru(pm3, lw['conv_w'], lw['conv_b'], lw['w_gate'], lw['b_gate'], lw['lam'], None)
        ao = _attention(pm3, lw['sink'], None, None, None)
    else:
        s_ret, s_lru, ck, cv = ctx
        ro, ret_s = _retention(pm3, lw['rd4'], ropes_ret, s_ret)
        lo, lru_s = _rglru(pm3, lw['conv_w'], lw['conv_b'], lw['w_gate'], lw['b_gate'], lw['lam'], s_lru)
        ao = _attention(pm3, lw['sink'], ropes_att, ck, cv)
    y = _merge_out(x2d, ro.reshape(B * L, D), lo.reshape(B * L, D), ao.reshape(B * L, D),
                   mod3, row_of_token, lw['g_pre'], lw['g_post'], lw['w_merge'], lw['b_merge'],
                   lw['w_branch'], lw['w_out'])
    return y.reshape(B, L, D), pm3, ret_s, lru_s


def _group_major(w, axis):
    shp = w.shape
    w = w.reshape(shp[:axis] + (ATT_KV_HEADS, ATT_GROUP, ATT_HD) + shp[axis + 1:])
    w = jnp.swapaxes(w, axis, axis + 1)
    return w.reshape(shp)


def _layer_weights(l, norm_pre, norm_post, w_in, ret_decay, conv_w, conv_b, lru_wa, lru_ba, lru_wx,
                   lru_bx, lru_lambda, att_sink, w_branch, w_merge, b_merge, w_out):
    offs = np.cumsum((512, 512, 1024, 1024, 1024, 1024, 1024, 256, 256, 1024))[:-1].tolist()
    rq, rk, rv, rg, lx, lgt, aq, ak, av, ag = jnp.split(w_in[l], offs, axis=-1)
    w_cat = jnp.concatenate([_group_major(aq, 1), _group_major(ag, 1), rv, rg, lx, lgt, rq, rk, ak, av],
                            axis=-1).astype(BF16)
    wb = jnp.stack([w_branch[l, 0], w_branch[l, 1], _group_major(w_branch[l, 2], 0)], axis=0).astype(BF16)
    w_gate = jnp.concatenate([lru_wa[l, 0], lru_wx[l, 0], lru_wa[l, 1], lru_wx[l, 1]], axis=-1).astype(BF16)
    b4 = jnp.stack([lru_ba[l, 0], lru_bx[l, 0], lru_ba[l, 1], lru_bx[l, 1]], axis=0)
    b_gate = b4.reshape(4, LRU_BLOCKS, LRU_BS).transpose(1, 0, 2).reshape(LRU_BLOCKS, 1, 4 * LRU_BS)
    rd4 = jnp.broadcast_to(ret_decay[l][:, :, None, None], (2, RET_HEADS, 1, RET_DV))
    return {
        'g_pre': norm_pre[l].reshape(1, D), 'g_post': norm_post[l].reshape(1, D),
        'w_in': w_cat, 'rd4': rd4,
        'conv_w': conv_w[l], 'conv_b': conv_b[l].reshape(1, D),
        'w_gate': w_gate, 'b_gate': b_gate, 'lam': lru_lambda[l],
        'sink': att_sink[l], 'w_merge': w_merge[l].astype(BF16), 'b_merge': b_merge[l].reshape(1, N_BRANCH * D),
        'w_branch': wb, 'w_out': w_out[l].astype(BF16),
    }


def kernel(x_prompt, x_sample, cache_k, cache_v, state_ret, state_lru, c, c_ctx, w_ada, b_ada, norm_pre, norm_post, w_in, ret_decay, conv_w, conv_b, lru_wa, lru_ba, lru_wx, lru_bx, lru_lambda, att_sink, w_branch, w_merge, b_merge, w_out):
    B, S, _ = x_prompt.shape
    Bd, Ld, _ = x_sample.shape
    P = cache_k.shape[2]
    cond8 = jnp.concatenate([c_ctx[None, :], c, jnp.zeros((8 - 1 - Bd, D), F32)], axis=0)
    mod = _ada_mod(cond8, w_ada, b_ada)
    ropes_ret = _rope_tables(Ld, RET_DK)
    ropes_att = _rope_tables(Ld, ATT_HD)

    yp, ys = x_prompt, x_sample
    new_k, new_v, new_ret, new_lru = [], [], [], []
    for l in range(DEPTH):
        lw = _layer_weights(l, norm_pre, norm_post, w_in, ret_decay, conv_w, conv_b, lru_wa, lru_ba,
                            lru_wx, lru_bx, lru_lambda, att_sink, w_branch, w_merge, b_merge, w_out)
        mod3 = mod[l].reshape(8, 1, 3 * D)
        yp, pm_p, r_s, l_s = _layer(yp, mod3, lambda t: 0, lw, None, None, None)
        new_ret.append(r_s)
        new_lru.append(l_s)
        new_k.append(pm_p[:, :, OFF_AK:OFF_AK + ATT_KV_W].reshape(B, S, ATT_KV_HEADS, ATT_HD))
        new_v.append(pm_p[:, :, OFF_AV:OFF_AV + ATT_KV_W].reshape(B, S, ATT_KV_HEADS, ATT_HD))
        ctx = (state_ret[:, l], state_lru[:, l],
               cache_k[:, l].reshape(Bd, P, ATT_KV_W), cache_v[:, l].reshape(Bd, P, ATT_KV_W))
        ys, _, _, _ = _layer(ys, mod3, lambda t: 1 + t // Ld, lw, ctx, ropes_ret, ropes_att)
    return (yp, ys, jnp.stack(new_k, axis=1), jnp.stack(new_v, axis=1),
            jnp.stack(new_ret, axis=1), jnp.stack(new_lru, axis=1))
```

```python
import functools

import numpy as np
import jax
import jax.numpy as jnp
from jax import lax
from jax.experimental import pallas as pl
from jax.experimental.pallas import tpu as pltpu

F32 = jnp.float32
BF16 = jnp.bfloat16

D = 1024
DEPTH = 2
GRID_W = 64
EPS = 1e-6
N_BRANCH = 3
RET_HEADS = 4
RET_DV = 256
RET_DK = 128
CHUNK = 128
LRU_BLOCKS = 8
LRU_BS = 128
LRU_C = 8.0
LRU_GROUP = 64
LRU_ITER_GROUPS = 4
LRU_BATCH_LONG = 2
LRU_BATCH_SHORT = 4
ATT_HD = 64
ATT_Q_HEADS = 16
ATT_KV_HEADS = 4
ATT_GROUP = 4
ATT_KV_W = 256
WINDOW = 128
ROPE_BASE = 10000.0

OFF_AQ = 0
OFF_AG = 1024
OFF_RV = 2048
OFF_RG = 3072
OFF_LX = 4096
OFF_LG = 5120
OFF_RQ = 6144
OFF_RK = 6656
OFF_AK = 7168
OFF_AV = 7424
W_IN = 7680

VMEM_LIMIT = 56 * 1024 * 1024


def _cparams(sem):
    return pltpu.CompilerParams(dimension_semantics=sem, vmem_limit_bytes=VMEM_LIMIT)


def _sigmoid(x):
    return 0.5 * jnp.tanh(0.5 * x) + 0.5


def _silu(x):
    return x * _sigmoid(x)


def _softplus(z):
    return jnp.maximum(z, 0.0) + jnp.log1p(jnp.exp(-jnp.abs(z)))


def _dot(a, b):
    return jnp.dot(a, b, preferred_element_type=F32)


def _dot_nt(a, b):
    return lax.dot_general(a, b, (((1,), (1,)), ((), ())), preferred_element_type=F32)


def _dot_tn(a, b):
    return lax.dot_general(a, b, (((0,), (0,)), ((), ())), preferred_element_type=F32)


def _modulated_norm(x, g, mod):
    y = x * lax.rsqrt(jnp.mean(x * x, axis=-1, keepdims=True) + EPS) * g
    return y * (1.0 + mod[:, D:2 * D]) + mod[:, 0:D]


def _ada_kernel(c_ref, w_ref, b_ref, o_ref):
    s = _silu(c_ref[...])
    o_ref[...] = _dot(s.astype(BF16), w_ref[...].astype(BF16)) + b_ref[...]


def _ada_mod(cond8, w_ada, b_ada):
    tn = 1024
    return pl.pallas_call(
        _ada_kernel,
        grid=(DEPTH, 3 * D // tn),
        in_specs=[
            pl.BlockSpec((8, D), lambda l, j: (0, 0)),
            pl.BlockSpec((None, D, tn), lambda l, j: (l, 0, j)),
            pl.BlockSpec((None, 1, tn), lambda l, j: (l, 0, j)),
        ],
        out_specs=pl.BlockSpec((None, 8, tn), lambda l, j: (l, 0, j)),
        out_shape=jax.ShapeDtypeStruct((DEPTH, 8, 3 * D), F32),
        compiler_params=_cparams(("arbitrary", "arbitrary")),
    )(cond8, w_ada, b_ada.reshape(DEPTH, 1, 3 * D))


IN_TM = 1024
IN_TN = 1280


def _in_kernel(x_ref, mod_ref, g_ref, w_ref, o_ref, h_all):
    j = pl.program_id(0)
    i = pl.program_id(1)
    r0 = pl.multiple_of(i * IN_TM, IN_TM)

    @pl.when(j == 0)
    def _():
        h = _modulated_norm(x_ref[...], g_ref[...], mod_ref[...])
        h_all[pl.ds(r0, IN_TM), :] = h.astype(BF16)

    o_ref[...] = _dot(h_all[pl.ds(r0, IN_TM), :], w_ref[...])


def _in_proj(x2d, mod3, row_fn, g_pre, w_in):
    T = x2d.shape[0]
    n_m = T // IN_TM
    return pl.pallas_call(
        _in_kernel,
        grid=(W_IN // IN_TN, n_m),
        in_specs=[
            pl.BlockSpec((IN_TM, D), lambda j, i: (jnp.where(j == 0, i, n_m - 1), 0)),
            pl.BlockSpec((None, 1, 3 * D), lambda j, i: (row_fn(i * IN_TM), 0, 0)),
            pl.BlockSpec((1, D), lambda j, i: (0, 0)),
            pl.BlockSpec((D, IN_TN), lambda j, i: (0, j)),
        ],
        out_specs=pl.BlockSpec((IN_TM, IN_TN), lambda j, i: (i, j)),
        out_shape=jax.ShapeDtypeStruct((T, W_IN), F32),
        scratch_shapes=[pltpu.VMEM((T, D), BF16)],
        compiler_params=_cparams(("arbitrary", "arbitrary")),
    )(x2d, mod3, g_pre, w_in)


def _rope(x, cos_t, sin_t, half):
    lane = lax.broadcasted_iota(jnp.int32, x.shape, 1)
    up = pltpu.roll(x, x.shape[1] - half, axis=1)
    dn = pltpu.roll(x, half, axis=1)
    partner = jnp.where((lane & half) == 0, up, dn)
    return x * cos_t + partner * sin_t


def _rope_tables(n_tokens, dim):
    nq = dim // 4
    lane = np.arange(128)
    within = lane % dim
    axis = within // (2 * nq)
    freq = within % nq
    sign = np.where((within % (2 * nq)) < nq, -1.0, 1.0).astype(np.float32)
    inv = ROPE_BASE ** (-jnp.arange(nq, dtype=F32) / nq)
    t = jnp.arange(n_tokens)
    pos = jnp.stack([(t // GRID_W).astype(F32), (t % GRID_W).astype(F32)], axis=1)
    ang = pos[:, axis] * inv[freq][None, :]
    return jnp.cos(ang), jnp.sin(ang) * sign[None, :]


def _ret_kernel(*refs, n_chunks, has_ctx, HB):
    if has_ctx:
        (q_ref, k_ref, v_ref, g_ref, rd_ref, cos_ref, sin_ref, s0_ref,
         o_ref, s_ref, qs, ks, s_all, sf_run, sb_run) = refs
    else:
        (q_ref, k_ref, v_ref, g_ref, rd_ref,
         o_ref, s_ref, qs, ks, s_all, sf_run, sb_run) = refs
    C = CHUNK
    ii = lax.broadcasted_iota(jnp.int32, (C, C), 0)
    jj = lax.broadcasted_iota(jnp.int32, (C, C), 1)
    diff = (ii - jj).astype(F32)
    ri = ii.astype(F32)

    steps = []
    for hh in range(HB):
        ksl = slice(hh * RET_DK, (hh + 1) * RET_DK)
        vsl = slice(hh * RET_DV, (hh + 1) * RET_DV)

        q = q_ref[:, ksl] * (RET_DK ** -0.5)
        k = k_ref[:, ksl]
        if has_ctx:
            q = _rope(q, cos_ref[...], sin_ref[...], 32)
            k = _rope(k, cos_ref[...], sin_ref[...], 32)
            sf_run[hh] = s0_ref[0, hh]
            sb_run[hh] = s0_ref[1, hh]
        else:
            sf_run[hh] = jnp.zeros((RET_DK, RET_DV), F32)
            sb_run[hh] = jnp.zeros((RET_DK, RET_DV), F32)
        qs[hh] = q
        ks[hh] = k

        lg_f = -_softplus(-rd_ref[0, hh])
        lg_b = -_softplus(-rd_ref[1, hh])
        lgf = lg_f[:, :C]
        lgb = lg_b[:, :C]
        dmat = (jnp.where(diff >= 0, jnp.exp(lgf * jnp.maximum(diff, 0.0)), 0.0)
                + jnp.where(diff <= 0, jnp.exp(lgb * jnp.maximum(-diff, 0.0)), 0.0))
        dq_f = jnp.exp(lgf * (ri + 1.0))
        dk_f = jnp.exp(lgf * (C - 1.0 - ri))
        dq_b = jnp.exp(lgb * (C - ri))
        dk_b = jnp.exp(lgb * ri)
        dc_f = jnp.exp(lg_f * float(C))
        dc_b = jnp.exp(lg_b * float(C))

        def state_step(t, carry, hh=hh, vsl=vsl, dk_f=dk_f, dk_b=dk_b, dc_f=dc_f, dc_b=dc_b):
            cf = t
            cb = n_chunks - 1 - t
            rf = pl.multiple_of(cf * C, C)
            rb = pl.multiple_of(cb * C, C)
            sf = sf_run[hh]
            sb = sb_run[hh]
            s_all[hh, cf, 0:RET_DK, :] = sf.astype(BF16)
            s_all[hh, cb, RET_DK:2 * RET_DK, :] = sb.astype(BF16)
            kf = (ks[hh, pl.ds(rf, C), :] * dk_f).astype(BF16)
            kb = (ks[hh, pl.ds(rb, C), :] * dk_b).astype(BF16)
            sf_run[hh] = sf * dc_f + _dot_tn(kf, v_ref[pl.ds(rf, C), vsl].astype(BF16))
            sb_run[hh] = sb * dc_b + _dot_tn(kb, v_ref[pl.ds(rb, C), vsl].astype(BF16))
            return carry

        def out_step(c, carry, hh=hh, vsl=vsl, dmat=dmat, dq_f=dq_f, dq_b=dq_b):
            r0 = pl.multiple_of(c * C, C)
            qc = qs[hh, pl.ds(r0, C), :]
            kc = ks[hh, pl.ds(r0, C), :]
            vc = v_ref[pl.ds(r0, C), vsl].astype(BF16)
            a = _dot_nt(qc.astype(BF16), kc.astype(BF16)) * dmat
            qq = jnp.concatenate([qc * dq_f, qc * dq_b], axis=1).astype(BF16)
            o = _dot(a.astype(BF16), vc) + _dot(qq, s_all[hh, c])
            mu = jnp.mean(o, axis=-1, keepdims=True)
            oc = o - mu
            on = oc * lax.rsqrt(jnp.mean(oc * oc, axis=-1, keepdims=True) + EPS)
            o_ref[pl.ds(r0, C), vsl] = on * _silu(g_ref[pl.ds(r0, C), vsl])
            return carry

        steps.append((state_step, out_step))

    def all_state(t, carry):
        for state_step, _ in steps:
            state_step(t, 0)
        return carry

    def all_out(c, carry):
        for _, out_step in steps:
            out_step(c, 0)
        return carry

    if n_chunks <= 2:
        for t in range(n_chunks):
            all_state(t, 0)
        for c in range(n_chunks):
            all_out(c, 0)
    else:
        lax.fori_loop(0, n_chunks, all_state, 0)
        lax.fori_loop(0, n_chunks, all_out, 0, unroll=2)
    for hh in range(HB):
        s_ref[0, hh] = sf_run[hh]
        s_ref[1, hh] = sb_run[hh]


def _retention(pm3, rd4, ropes, s0):
    B, L, _ = pm3.shape
    n_chunks = L // CHUNK
    has_ctx = s0 is not None
    H = RET_HEADS
    HB = 2 if n_chunks > 2 else H
    kw, vw = HB * RET_DK, HB * RET_DV
    in_specs = [
        pl.BlockSpec((None, L, kw), lambda b, h: (b, 0, OFF_RQ // kw + h)),
        pl.BlockSpec((None, L, kw), lambda b, h: (b, 0, OFF_RK // kw + h)),
        pl.BlockSpec((None, L, vw), lambda b, h: (b, 0, OFF_RV // vw + h)),
        pl.BlockSpec((None, L, vw), lambda b, h: (b, 0, OFF_RG // vw + h)),
        pl.BlockSpec((2, HB, 1, RET_DV), lambda b, h: (0, h, 0, 0)),
    ]
    args = [pm3, pm3, pm3, pm3, rd4]
    if has_ctx:
        in_specs += [
            pl.BlockSpec((L, 128), lambda b, h: (0, 0)),
            pl.BlockSpec((L, 128), lambda b, h: (0, 0)),
            pl.BlockSpec((None, 2, HB, RET_DK, RET_DV), lambda b, h: (b, 0, h, 0, 0)),
        ]
        args += [ropes[0], ropes[1], s0]
    return pl.pallas_call(
        functools.partial(_ret_kernel, n_chunks=n_chunks, has_ctx=has_ctx, HB=HB),
        grid=(B, H // HB),
        in_specs=in_specs,
        out_specs=[
            pl.BlockSpec((None, L, vw), lambda b, h: (b, 0, h)),
            pl.BlockSpec((None, 2, HB, RET_DK, RET_DV), lambda b, h: (b, 0, h, 0, 0)),
        ],
        out_shape=[
            jax.ShapeDtypeStruct((B, L, D), F32),
            jax.ShapeDtypeStruct((B, 2, H, RET_DK, RET_DV), F32),
        ],
        scratch_shapes=[
            pltpu.VMEM((HB, L, RET_DK), F32),
            pltpu.VMEM((HB, L, RET_DK), F32),
            pltpu.VMEM((HB, n_chunks, 2 * RET_DK, RET_DV), BF16),
            pltpu.VMEM((HB, RET_DK, RET_DV), F32),
            pltpu.VMEM((HB, RET_DK, RET_DV), F32),
        ],
        compiler_params=_cparams(("arbitrary", "arbitrary")),
    )(*args)


def _lru_kernel(*refs, L, has_ctx):
    if has_ctx:
        (x_ref, gt_ref, cw_ref, cb_ref, w_ref, b_ref, lam_ref, h0_ref,
         o_ref, s_ref, xc_scr, hf_scr, pb_scr, ub_scr) = refs
    else:
        (x_ref, gt_ref, cw_ref, cb_ref, w_ref, b_ref, lam_ref,
         o_ref, s_ref, xc_scr, hf_scr, pb_scr, ub_scr) = refs
    G = LRU_GROUP
    NG = LRU_ITER_GROUPS
    n_it = L // (NG * G)
    W = LRU_BS
    BB = x_ref.shape[0]

    row = lax.broadcasted_iota(jnp.int32, (L, W), 0)
    for bb in range(BB):
        x = x_ref[bb]
        xm1 = jnp.where(row >= 1, pltpu.roll(x, 1, axis=0), 0.0)
        xp1 = jnp.where(row < L - 1, pltpu.roll(x, L - 1, axis=0), 0.0)
        xp2 = jnp.where(row < L - 2, pltpu.roll(x, L - 2, axis=0), 0.0)
        xc_scr[bb] = (cw_ref[0:1, :] * xm1 + cw_ref[1:2, :] * x + cw_ref[2:3, :] * xp1
                      + cw_ref[3:4, :] * xp2 + cb_ref[...])

    nsp_f = -LRU_C * _softplus(-lam_ref[0:1, :])
    nsp_b = -LRU_C * _softplus(-lam_ref[1:2, :])
    row8 = lax.broadcasted_iota(jnp.int32, (8, W), 0)

    def gates(zr, zi, nsp, xc):
        log_a = _sigmoid(zr) * nsp
        a = jnp.exp(log_a)
        u = jnp.sqrt(-jnp.tanh(log_a) * (a * a + 1.0)) * (_sigmoid(zi) * xc)
        return a, u

    def seg_scan(p, hl, carry, reverse):
        a, u = p, hl
        for s in (1, 2, 4):
            sh = (8 - s) if reverse else s
            a_sh = pltpu.roll(a, sh, axis=0)
            u_sh = pltpu.roll(u, sh, axis=0)
            m = (row8 < 8 - s) if reverse else (row8 >= s)
            u = jnp.where(m, a * u_sh + u, u)
            a = jnp.where(m, a * a_sh, a)
        h_end = u + a * carry
        if reverse:
            h_in = jnp.where(row8 == 7, carry, pltpu.roll(h_end, 7, axis=0))
            return h_in, h_end[0:1]
        h_in = jnp.where(row8 == 0, carry, pltpu.roll(h_end, 1, axis=0))
        return h_in, h_end[7:8]

    def pass1(it, carries):
        return tuple(pass1_one(it, carries[bb], xc_scr.at[bb], hf_scr.at[bb], pb_scr.at[bb], ub_scr.at[bb])
                     for bb in range(BB))

    def pass2(t, carries):
        return tuple(pass2_one(t, carries[bb], gt_ref.at[bb], o_ref.at[bb], hf_scr.at[bb], pb_scr.at[bb],
                               ub_scr.at[bb]) for bb in range(BB))

    def pass1_one(it, carry, xc_scr, hf_scr, pb_scr, ub_scr):
        base = pl.multiple_of(it * (NG * G), NG * G)
        xg = jnp.concatenate([xc_scr[pl.ds(base + gi * G + r, 8, stride=8), :]
                              for gi in range(NG) for r in range(8)], axis=0)
        zz = _dot(xg.astype(BF16), w_ref[...]) + b_ref[...]
        a_f, u_f = gates(zz[:, 0:W], zz[:, W:2 * W], nsp_f, xg)
        a_b, u_b = gates(zz[:, 2 * W:3 * W], zz[:, 3 * W:4 * W], nsp_b, xg)
        for gi in range(NG):
            v = lambda arr, r: arr[gi * G + r * 8:gi * G + (r + 1) * 8]
            p = [v(a_f, 0)]
            hl = [v(u_f, 0)]
            for r in range(1, 8):
                hl.append(v(a_f, r) * hl[-1] + v(u_f, r))
                p.append(v(a_f, r) * p[-1])
            h_in, carry = seg_scan(p[7], hl[7], carry, False)
            hf_scr[pl.ds(base + gi * G, G), :] = jnp.concatenate(
                [hl[r] + p[r] * h_in for r in range(8)], axis=0)
            pb = [None] * 8
            hb = [None] * 8
            pb[7] = v(a_b, 7)
            hb[7] = v(u_b, 7)
            for r in range(6, -1, -1):
                hb[r] = v(a_b, r) * hb[r + 1] + v(u_b, r)
                pb[r] = v(a_b, r) * pb[r + 1]
            pb_scr[pl.ds(base + gi * G, G), :] = jnp.concatenate(pb, axis=0)
            ub_scr[pl.ds(base + gi * G, G), :] = jnp.concatenate(hb, axis=0)
        return carry

    def pass2_one(t, carry, gt_ref, o_ref, hf_scr, pb_scr, ub_scr):
        base = pl.multiple_of((n_it - 1 - t) * (NG * G), NG * G)
        for gi in range(NG - 1, -1, -1):
            gb = base + gi * G
            pb = pb_scr[pl.ds(gb, G), :]
            hb = ub_scr[pl.ds(gb, G), :]
            hf = hf_scr[pl.ds(gb, G), :]
            h_in, carry = seg_scan(pb[0:8], hb[0:8], carry, True)
            for r in range(8):
                sl = slice(r * 8, (r + 1) * 8)
                gt = gt_ref[pl.ds(gb + r, 8, stride=8), :]
                o_ref[pl.ds(gb + r, 8, stride=8), :] = (hf[sl] + hb[sl] + pb[sl] * h_in) * _silu(gt)
        return carry

    if has_ctx:
        h0f = tuple(h0_ref[bb, 0:1, :] for bb in range(BB))
        h0b = tuple(h0_ref[bb, 1:2, :] for bb in range(BB))
    else:
        h0f = h0b = tuple(jnp.zeros((1, W), F32) for _ in range(BB))
    if n_it == 1:
        lf = pass1(0, h0f)
        lb = pass2(0, h0b)
    else:
        lf = lax.fori_loop(0, n_it, pass1, h0f)
        lb = lax.fori_loop(0, n_it, pass2, h0b)
    for bb in range(BB):
        s_ref[bb, 0:1, :] = lf[bb]
        s_ref[bb, 1:2, :] = lb[bb]


def _rglru(pm3, conv_w, conv_b, w_gate, b_gate, lam, h0):
    B, L, _ = pm3.shape
    has_ctx = h0 is not None
    W = LRU_BS
    BB = LRU_BATCH_SHORT if L <= LRU_ITER_GROUPS * LRU_GROUP else LRU_BATCH_LONG
    in_specs = [
        pl.BlockSpec((BB, L, W), lambda b, n: (b, 0, OFF_LX // W + n)),
        pl.BlockSpec((BB, L, W), lambda b, n: (b, 0, OFF_LG // W + n)),
        pl.BlockSpec((4, W), lambda b, n: (0, n)),
        pl.BlockSpec((1, W), lambda b, n: (0, n)),
        pl.BlockSpec((None, W, 4 * W), lambda b, n: (n, 0, 0)),
        pl.BlockSpec((None, 1, 4 * W), lambda b, n: (n, 0, 0)),
        pl.BlockSpec((2, W), lambda b, n: (0, n)),
    ]
    args = [pm3, pm3, conv_w, conv_b, w_gate, b_gate, lam]
    if has_ctx:
        in_specs.append(pl.BlockSpec((BB, 2, W), lambda b, n: (b, 0, n)))
        args.append(h0)
    return pl.pallas_call(
        functools.partial(_lru_kernel, L=L, has_ctx=has_ctx),
        grid=(B // BB, LRU_BLOCKS),
        in_specs=in_specs,
        out_specs=[
            pl.BlockSpec((BB, L, W), lambda b, n: (b, 0, n)),
            pl.BlockSpec((BB, 2, W), lambda b, n: (b, 0, n)),
        ],
        out_shape=[
            jax.ShapeDtypeStruct((B, L, D), F32),
            jax.ShapeDtypeStruct((B, 2, D), F32),
        ],
        scratch_shapes=[pltpu.VMEM((BB, L, W), F32)] * 4,
        compiler_params=_cparams(("arbitrary", "arbitrary")),
    )(*args)


def _att_kernel(*refs, L, Q, NB, has_ctx):
    if NB == 1:
        _att_one(*refs, L=L, Q=Q, has_ctx=has_ctx)
    else:
        for bb in range(NB):
            _att_one(refs[0], *[r.at[bb] for r in refs[1:]], L=L, Q=Q, has_ctx=has_ctx)


def _att_one(*refs, L, Q, has_ctx):
    if has_ctx:
        (sink_ref, q_ref, ag_ref, k_ref, v_ref, ck_ref, cv_ref, cosq_ref, sinq_ref, cosk_ref, sink_tab_ref,
         o_ref, km, vm, ckm, cvm) = refs
    else:
        (sink_ref, q_ref, ag_ref, k_ref, v_ref, o_ref, km, vm) = refs
    G = ATT_GROUP
    KW = ATT_KV_W
    qi = pl.program_id(1)

    def head_masked(dst, val):
        head = lax.shift_right_logical(lax.broadcasted_iota(jnp.int32, val.shape, 1), 6)
        for h in range(ATT_KV_HEADS):
            dst[h] = jnp.where(head == h, val, 0.0).astype(BF16)

    @pl.when(qi == 0)
    def _():
        kk = k_ref[...]
        if has_ctx:
            kk = _rope(kk, jnp.concatenate([cosk_ref[...]] * 2, axis=1),
                       jnp.concatenate([sink_tab_ref[...]] * 2, axis=1), 16)
            head_masked(ckm, ck_ref[...])
            head_masked(cvm, cv_ref[...])
        head_masked(km, kk)
        head_masked(vm, v_ref[...])

    q = q_ref[...] * (ATT_HD ** -0.5)
    if has_ctx:
        q = _rope(q, jnp.concatenate([cosq_ref[...]] * 8, axis=1),
                  jnp.concatenate([sinq_ref[...]] * 8, axis=1), 16)
        span = Q + 2 * WINDOW
        start = pl.multiple_of(jnp.clip((qi - 1) * Q, 0, L - span), Q)
        qpos = qi * Q + lax.broadcasted_iota(jnp.int32, (Q, span), 0)
        kpos = start + lax.broadcasted_iota(jnp.int32, (Q, span), 1)
        band = jnp.abs(qpos - kpos) <= WINDOW
        band4 = jnp.concatenate([band] * G, axis=0)
    qg = jnp.concatenate([q[:, g * KW:(g + 1) * KW] for g in range(G)], axis=0).astype(BF16)

    o = None
    for h in range(ATT_KV_HEADS):
        sk = jnp.concatenate(
            [jnp.full((Q, 1), sink_ref[h * G + g], F32) for g in range(G)], axis=0)
        if has_ctx:
            s = jnp.concatenate([jnp.where(band4, _dot_nt(qg, km[h, pl.ds(start, span), :]), -jnp.inf),
                                 _dot_nt(qg, ckm[h])], axis=1)
            m = jnp.maximum(jnp.max(s, axis=-1, keepdims=True), sk)
            p = jnp.exp(s - m)
            den = jnp.sum(p, axis=-1, keepdims=True) + jnp.exp(sk - m)
            pb = p.astype(BF16)
            oh = _dot(pb[:, :span], vm[h, pl.ds(start, span), :]) + _dot(pb[:, span:], cvm[h])
        else:
            s1 = _dot_nt(qg, km[h])
            m = jnp.maximum(jnp.max(s1, axis=-1, keepdims=True), sk)
            p1 = jnp.exp(s1 - m)
            den = jnp.sum(p1, axis=-1, keepdims=True) + jnp.exp(sk - m)
            oh = _dot(p1.astype(BF16), vm[h])
        oh = oh * (1.0 / den)
        o = oh if o is None else o + oh
    ao = jnp.concatenate([o[g * Q:(g + 1) * Q] for g in range(G)], axis=1)
    o_ref[...] = ao * _silu(ag_ref[...])


def _attention(pm3, sink, ropes, ck, cv):
    B, L, _ = pm3.shape
    has_ctx = ck is not None
    Q = CHUNK if has_ctx else L
    NB = 1 if has_ctx else 2
    nb = None if NB == 1 else NB
    lead = () if NB == 1 else (NB,)
    KW = ATT_KV_W
    in_specs = [
        pl.BlockSpec(memory_space=pltpu.SMEM),
        pl.BlockSpec((nb, Q, D), lambda b, i: (b, i, OFF_AQ // D)),
        pl.BlockSpec((nb, Q, D), lambda b, i: (b, i, OFF_AG // D)),
        pl.BlockSpec((nb, L, KW), lambda b, i: (b, 0, OFF_AK // KW)),
        pl.BlockSpec((nb, L, KW), lambda b, i: (b, 0, OFF_AV // KW)),
    ]
    args = [sink, pm3, pm3, pm3, pm3]
    scratch = [pltpu.VMEM(lead + (ATT_KV_HEADS, L, KW), BF16), pltpu.VMEM(lead + (ATT_KV_HEADS, L, KW), BF16)]
    if has_ctx:
        P = ck.shape[1]
        in_specs += [
            pl.BlockSpec((None, P, KW), lambda b, i: (b, 0, 0)),
            pl.BlockSpec((None, P, KW), lambda b, i: (b, 0, 0)),
            pl.BlockSpec((Q, 128), lambda b, i: (i, 0)),
            pl.BlockSpec((Q, 128), lambda b, i: (i, 0)),
            pl.BlockSpec((L, 128), lambda b, i: (0, 0)),
            pl.BlockSpec((L, 128), lambda b, i: (0, 0)),
        ]
        args += [ck, cv, ropes[0], ropes[1], ropes[0], ropes[1]]
        scratch += [pltpu.VMEM((ATT_KV_HEADS, P, KW), BF16), pltpu.VMEM((ATT_KV_HEADS, P, KW), BF16)]
    return pl.pallas_call(
        functools.partial(_att_kernel, L=L, Q=Q, NB=NB, has_ctx=has_ctx),
        grid=(B // NB, L // Q),
        in_specs=in_specs,
        out_specs=pl.BlockSpec((nb, Q, D), lambda b, i: (b, i, 0)),
        out_shape=jax.ShapeDtypeStruct((B, L, D), F32),
        scratch_shapes=scratch,
        compiler_params=_cparams(("arbitrary", "arbitrary")),
    )(*args)


OUT_TM = 512


def _out_kernel(x_ref, ro_ref, lo_ref, ao_ref, mod_ref, gpre_ref, gpost_ref,
                wm_ref, bm_ref, wb_ref, wo_ref, y_ref):
    x = x_ref[...]
    mod = mod_ref[...]
    h = _modulated_norm(x, gpre_ref[...], mod).astype(BF16)
    merged = None
    for n, br_ref in enumerate((ro_ref, lo_ref, ao_ref)):
        gate = _sigmoid(_dot(h, wm_ref[:, n * D:(n + 1) * D]) + bm_ref[:, n * D:(n + 1) * D])
        z = gate * _dot(br_ref[...].astype(BF16), wb_ref[n])
        merged = z if merged is None else merged + z
    out = _dot(merged.astype(BF16), wo_ref[...])
    nrm = out * lax.rsqrt(jnp.mean(out * out, axis=-1, keepdims=True) + EPS) * gpost_ref[...]
    y_ref[...] = x + mod[:, 2 * D:3 * D] * nrm


def _merge_out(x2d, ro, lo, ao, mod3, row_fn, g_pre, g_post, w_merge, b_merge, w_branch, w_out):
    T = x2d.shape[0]
    tm = OUT_TM
    tok = lambda i: (i, 0)
    once = pl.Buffered(1)
    return pl.pallas_call(
        _out_kernel,
        grid=(T // tm,),
        in_specs=[
            pl.BlockSpec((tm, D), tok),
            pl.BlockSpec((tm, D), tok),
            pl.BlockSpec((tm, D), tok),
            pl.BlockSpec((tm, D), tok),
            pl.BlockSpec((None, 1, 3 * D), lambda i: (row_fn(i * tm), 0, 0)),
            pl.BlockSpec((1, D), lambda i: (0, 0)),
            pl.BlockSpec((1, D), lambda i: (0, 0)),
            pl.BlockSpec((D, N_BRANCH * D), lambda i: (0, 0), pipeline_mode=once),
            pl.BlockSpec((1, N_BRANCH * D), lambda i: (0, 0)),
            pl.BlockSpec((N_BRANCH, D, D), lambda i: (0, 0, 0), pipeline_mode=once),
            pl.BlockSpec((D, D), lambda i: (0, 0), pipeline_mode=once),
        ],
        out_specs=pl.BlockSpec((tm, D), tok),
        out_shape=jax.ShapeDtypeStruct((T, D), F32),
        compiler_params=_cparams(("arbitrary",)),
    )(x2d, ro, lo, ao, mod3, g_pre, g_post, w_merge, b_merge, w_branch, w_out)


def _layer(x, mod3, row_of_token, lw, ctx, ropes_ret, ropes_att):
    B, L, _ = x.shape
    x2d = x.reshape(B * L, D)
    pm2 = _in_proj(x2d, mod3, row_of_token, lw['g_pre'], lw['w_in'])
    pm3 = pm2.reshape(B, L, W_IN)
    if ctx is None:
        ro, ret_s = _retention(pm3, lw['rd4'], None, None)
        lo, lru_s = _rglru(pm3, lw['conv_w'], lw['conv_b'], lw['w_gate'], lw['b_gate'], lw['lam'], None)
        ao = _attention(pm3, lw['sink'], None, None, None)
    else:
        s_ret, s_lru, ck, cv = ctx
        ro, ret_s = _retention(pm3, lw['rd4'], ropes_ret, s_ret)
        lo, lru_s = _rglru(pm3, lw['conv_w'], lw['conv_b'], lw['w_gate'], lw['b_gate'], lw['lam'], s_lru)
        ao = _attention(pm3, lw['sink'], ropes_att, ck, cv)
    y = _merge_out(x2d, ro.reshape(B * L, D), lo.reshape(B * L, D), ao.reshape(B * L, D),
                   mod3, row_of_token, lw['g_pre'], lw['g_post'], lw['w_merge'], lw['b_merge'],
                   lw['w_branch'], lw['w_out'])
    return y.reshape(B, L, D), pm3, ret_s, lru_s


def _group_major(w, axis):
    shp = w.shape
    w = w.reshape(shp[:axis] + (ATT_KV_HEADS, ATT_GROUP, ATT_HD) + shp[axis + 1:])
    w = jnp.swapaxes(w, axis, axis + 1)
    return w.reshape(shp)


def _layer_weights(l, norm_pre, norm_post, w_in, ret_decay, conv_w, conv_b, lru_wa, lru_ba, lru_wx,
                   lru_bx, lru_lambda, att_sink, w_branch, w_merge, b_merge, w_out):
    offs = np.cumsum((512, 512, 1024, 1024, 1024, 1024, 1024, 256, 256, 1024))[:-1].tolist()
    rq, rk, rv, rg, lx, lgt, aq, ak, av, ag = jnp.split(w_in[l], offs, axis=-1)
    w_cat = jnp.concatenate([_group_major(aq, 1), _group_major(ag, 1), rv, rg, lx, lgt, rq, rk, ak, av],
                            axis=-1).astype(BF16)
    wb = jnp.stack([w_branch[l, 0], w_branch[l, 1], _group_major(w_branch[l, 2], 0)], axis=0).astype(BF16)
    w_gate = jnp.concatenate([lru_wa[l, 0], lru_wx[l, 0], lru_wa[l, 1], lru_wx[l, 1]], axis=-1).astype(BF16)
    b4 = jnp.stack([lru_ba[l, 0], lru_bx[l, 0], lru_ba[l, 1], lru_bx[l, 1]], axis=0)
    b_gate = b4.reshape(4, LRU_BLOCKS, LRU_BS).transpose(1, 0, 2).reshape(LRU_BLOCKS, 1, 4 * LRU_BS)
    rd4 = jnp.broadcast_to(ret_decay[l][:, :, None, None], (2, RET_HEADS, 1, RET_DV))
    return {
        'g_pre': norm_pre[l].reshape(1, D), 'g_post': norm_post[l].reshape(1, D),
        'w_in': w_cat, 'rd4': rd4,
        'conv_w': conv_w[l], 'conv_b': conv_b[l].reshape(1, D),
        'w_gate': w_gate, 'b_gate': b_gate, 'lam': lru_lambda[l],
        'sink': att_sink[l], 'w_merge': w_merge[l].astype(BF16), 'b_merge': b_merge[l].reshape(1, N_BRANCH * D),
        'w_branch': wb, 'w_out': w_out[l].astype(BF16),
    }


def kernel(x_prompt, x_sample, cache_k, cache_v, state_ret, state_lru, c, c_ctx, w_ada, b_ada, norm_pre, norm_post, w_in, ret_decay, conv_w, conv_b, lru_wa, lru_ba, lru_wx, lru_bx, lru_lambda, att_sink, w_branch, w_merge, b_merge, w_out):
    B, S, _ = x_prompt.shape
    Bd, Ld, _ = x_sample.shape
    P = cache_k.shape[2]
    cond8 = jnp.concatenate([c_ctx[None, :], c, jnp.zeros((8 - 1 - Bd, D), F32)], axis=0)
    mod = _ada_mod(cond8, w_ada, b_ada)
    ropes_ret = _rope_tables(Ld, RET_DK)
    ropes_att = _rope_tables(Ld, ATT_HD)

    yp, ys = x_prompt, x_sample
    new_k, new_v, new_ret, new_lru = [], [], [], []
    for l in range(DEPTH):
        lw = _layer_weights(l, norm_pre, norm_post, w_in, ret_decay, conv_w, conv_b, lru_wa, lru_ba,
                            lru_wx, lru_bx, lru_lambda, att_sink, w_branch, w_merge, b_merge, w_out)
        mod3 = mod[l].reshape(8, 1, 3 * D)
        yp, pm_p, r_s, l_s = _layer(yp, mod3, lambda t: 0, lw, None, None, None)
        new_ret.append(r_s)
        new_lru.append(l_s)
        new_k.append(pm_p[:, :, OFF_AK:OFF_AK + ATT_KV_W].reshape(B, S, ATT_KV_HEADS, ATT_HD))
        new_v.append(pm_p[:, :, OFF_AV:OFF_AV + ATT_KV_W].reshape(B, S, ATT_KV_HEADS, ATT_HD))
        ctx = (state_ret[:, l], state_lru[:, l],
               cache_k[:, l].reshape(Bd, P, ATT_KV_W), cache_v[:, l].reshape(Bd, P, ATT_KV_W))
        ys, _, _, _ = _layer(ys, mod3, lambda t: 1 + t // Ld, lw, ctx, ropes_ret, ropes_att)
    return (yp, ys, jnp.stack(new_k, axis=1), jnp.stack(new_v, axis=1),
            jnp.stack(new_ret, axis=1), jnp.stack(new_lru, axis=1))
```

```python
import functools

import numpy as np
import jax
import jax.numpy as jnp
from jax import lax
from jax.experimental import pallas as pl
from jax.experimental.pallas import tpu as pltpu

F32 = jnp.float32
BF16 = jnp.bfloat16

D = 1024
DEPTH = 2
GRID_W = 64
EPS = 1e-6
N_BRANCH = 3
RET_HEADS = 4
RET_DV = 256
RET_DK = 128
CHUNK = 128
LRU_BLOCKS = 8
LRU_BS = 128
LRU_C = 8.0
LRU_GROUP = 64
LRU_ITER_GROUPS = 4
LRU_BATCH_LONG = 2
LRU_BATCH_SHORT = 4
ATT_HD = 64
ATT_Q_HEADS = 16
ATT_KV_HEADS = 4
ATT_GROUP = 4
ATT_KV_W = 256
WINDOW = 128
ROPE_BASE = 10000.0

OFF_AQ = 0
OFF_AG = 1024
OFF_RV = 2048
OFF_RG = 3072
OFF_LX = 4096
OFF_LG = 5120
OFF_RQ = 6144
OFF_RK = 6656
OFF_AK = 7168
OFF_AV = 7424
W_IN = 7680

VMEM_LIMIT = 56 * 1024 * 1024


def _cparams(sem):
    return pltpu.CompilerParams(dimension_semantics=sem, vmem_limit_bytes=VMEM_LIMIT)


def _sigmoid(x):
    return 0.5 * jnp.tanh(0.5 * x) + 0.5


def _silu(x):
    return x * _sigmoid(x)


def _softplus(z):
    return jnp.maximum(z, 0.0) + jnp.log1p(jnp.exp(-jnp.abs(z)))


def _dot(a, b):
    return jnp.dot(a, b, preferred_element_type=F32)


def _dot_nt(a, b):
    return lax.dot_general(a, b, (((1,), (1,)), ((), ())), preferred_element_type=F32)


def _dot_tn(a, b):
    return lax.dot_general(a, b, (((0,), (0,)), ((), ())), preferred_element_type=F32)


def _modulated_norm(x, g, mod):
    y = x * lax.rsqrt(jnp.mean(x * x, axis=-1, keepdims=True) + EPS) * g
    return y * (1.0 + mod[:, D:2 * D]) + mod[:, 0:D]


def _ada_kernel(c_ref, w_ref, b_ref, o_ref):
    s = _silu(c_ref[...])
    o_ref[...] = _dot(s.astype(BF16), w_ref[...].astype(BF16)) + b_ref[...]


def _ada_mod(cond8, w_ada, b_ada):
    tn = 1024
    return pl.pallas_call(
        _ada_kernel,
        grid=(DEPTH, 3 * D // tn),
        in_specs=[
            pl.BlockSpec((8, D), lambda l, j: (0, 0)),
            pl.BlockSpec((None, D, tn), lambda l, j: (l, 0, j)),
            pl.BlockSpec((None, 1, tn), lambda l, j: (l, 0, j)),
        ],
        out_specs=pl.BlockSpec((None, 8, tn), lambda l, j: (l, 0, j)),
        out_shape=jax.ShapeDtypeStruct((DEPTH, 8, 3 * D), F32),
        compiler_params=_cparams(("arbitrary", "arbitrary")),
    )(cond8, w_ada, b_ada.reshape(DEPTH, 1, 3 * D))


IN_TM = 1024
IN_TN = 1280


def _in_kernel(x_ref, mod_ref, g_ref, w_ref, o_ref, h_all):
    j = pl.program_id(0)
    i = pl.program_id(1)
    r0 = pl.multiple_of(i * IN_TM, IN_TM)

    @pl.when(j == 0)
    def _():
        h = _modulated_norm(x_ref[...], g_ref[...], mod_ref[...])
        h_all[pl.ds(r0, IN_TM), :] = h.astype(BF16)

    o_ref[...] = _dot(h_all[pl.ds(r0, IN_TM), :], w_ref[...])


def _in_proj(x2d, mod3, row_fn, g_pre, w_in):
    T = x2d.shape[0]
    n_m = T // IN_TM
    return pl.pallas_call(
        _in_kernel,
        grid=(W_IN // IN_TN, n_m),
        in_specs=[
            pl.BlockSpec((IN_TM, D), lambda j, i: (jnp.where(j == 0, i, n_m - 1), 0)),
            pl.BlockSpec((None, 1, 3 * D), lambda j, i: (row_fn(i * IN_TM), 0, 0)),
            pl.BlockSpec((1, D), lambda j, i: (0, 0)),
            pl.BlockSpec((D, IN_TN), lambda j, i: (0, j)),
        ],
        out_specs=pl.BlockSpec((IN_TM, IN_TN), lambda j, i: (i, j)),
        out_shape=jax.ShapeDtypeStruct((T, W_IN), F32),
        scratch_shapes=[pltpu.VMEM((T, D), BF16)],
        compiler_params=_cparams(("arbitrary", "arbitrary")),
    )(x2d, mod3, g_pre, w_in)


def _rope(x, cos_t, sin_t, half):
    lane = lax.broadcasted_iota(jnp.int32, x.shape, 1)
    up = pltpu.roll(x, x.shape[1] - half, axis=1)
    dn = pltpu.roll(x, half, axis=1)
    partner = jnp.where((lane & half) == 0, up, dn)
    return x * cos_t + partner * sin_t


def _rope_tables(n_tokens, dim):
    nq = dim // 4
    lane = np.arange(128)
    within = lane % dim
    axis = within // (2 * nq)
    freq = within % nq
    sign = np.where((within % (2 * nq)) < nq, -1.0, 1.0).astype(np.float32)
    inv = ROPE_BASE ** (-jnp.arange(nq, dtype=F32) / nq)
    t = jnp.arange(n_tokens)
    pos = jnp.stack([(t // GRID_W).astype(F32), (t % GRID_W).astype(F32)], axis=1)
    ang = pos[:, axis] * inv[freq][None, :]
    return jnp.cos(ang), jnp.sin(ang) * sign[None, :]


def _ret_kernel(*refs, n_chunks, has_ctx, HB):
    if has_ctx:
        (q_ref, k_ref, v_ref, g_ref, rd_ref, cos_ref, sin_ref, s0_ref,
         o_ref, s_ref, qs, ks, s_all, sf_run, sb_run) = refs
    else:
        (q_ref, k_ref, v_ref, g_ref, rd_ref,
         o_ref, s_ref, qs, ks, s_all, sf_run, sb_run) = refs
    C = CHUNK
    ii = lax.broadcasted_iota(jnp.int32, (C, C), 0)
    jj = lax.broadcasted_iota(jnp.int32, (C, C), 1)
    diff = (ii - jj).astype(F32)
    ri = ii.astype(F32)

    steps = []
    for hh in range(HB):
        ksl = slice(hh * RET_DK, (hh + 1) * RET_DK)
        vsl = slice(hh * RET_DV, (hh + 1) * RET_DV)

        q = q_ref[:, ksl] * (RET_DK ** -0.5)
        k = k_ref[:, ksl]
        if has_ctx:
            q = _rope(q, cos_ref[...], sin_ref[...], 32)
            k = _rope(k, cos_ref[...], sin_ref[...], 32)
            sf_run[hh] = s0_ref[0, hh]
            sb_run[hh] = s0_ref[1, hh]
        else:
            sf_run[hh] = jnp.zeros((RET_DK, RET_DV), F32)
            sb_run[hh] = jnp.zeros((RET_DK, RET_DV), F32)
        qs[hh] = q
        ks[hh] = k

        lg_f = -_softplus(-rd_ref[0, hh])
        lg_b = -_softplus(-rd_ref[1, hh])
        lgf = lg_f[:, :C]
        lgb = lg_b[:, :C]
        dmat = (jnp.where(diff >= 0, jnp.exp(lgf * jnp.maximum(diff, 0.0)), 0.0)
                + jnp.where(diff <= 0, jnp.exp(lgb * jnp.maximum(-diff, 0.0)), 0.0))
        dq_f = jnp.exp(lgf * (ri + 1.0))
        dk_f = jnp.exp(lgf * (C - 1.0 - ri))
        dq_b = jnp.exp(lgb * (C - ri))
        dk_b = jnp.exp(lgb * ri)
        dc_f = jnp.exp(lg_f * float(C))
        dc_b = jnp.exp(lg_b * float(C))

        def state_step(t, carry, hh=hh, vsl=vsl, dk_f=dk_f, dk_b=dk_b, dc_f=dc_f, dc_b=dc_b):
            cf = t
            cb = n_chunks - 1 - t
            rf = pl.multiple_of(cf * C, C)
            rb = pl.multiple_of(cb * C, C)
            sf = sf_run[hh]
            sb = sb_run[hh]
            s_all[hh, cf, 0:RET_DK, :] = sf.astype(BF16)
            s_all[hh, cb, RET_DK:2 * RET_DK, :] = sb.astype(BF16)
            kf = (ks[hh, pl.ds(rf, C), :] * dk_f).astype(BF16)
            kb = (ks[hh, pl.ds(rb, C), :] * dk_b).astype(BF16)
            sf_run[hh] = sf * dc_f + _dot_tn(kf, v_ref[pl.ds(rf, C), vsl].astype(BF16))
            sb_run[hh] = sb * dc_b + _dot_tn(kb, v_ref[pl.ds(rb, C), vsl].astype(BF16))
            return carry

        def out_step(c, carry, hh=hh, vsl=vsl, dmat=dmat, dq_f=dq_f, dq_b=dq_b):
            r0 = pl.multiple_of(c * C, C)
            qc = qs[hh, pl.ds(r0, C), :]
            kc = ks[hh, pl.ds(r0, C), :]
            vc = v_ref[pl.ds(r0, C), vsl].astype(BF16)
            a = _dot_nt(qc.astype(BF16), kc.astype(BF16)) * dmat
            qq = jnp.concatenate([qc * dq_f, qc * dq_b], axis=1).astype(BF16)
            o = _dot(a.astype(BF16), vc) + _dot(qq, s_all[hh, c])
            mu = jnp.mean(o, axis=-1, keepdims=True)
            oc = o - mu
            on = oc * lax.rsqrt(jnp.mean(oc * oc, axis=-1, keepdims=True) + EPS)
            o_ref[pl.ds(r0, C), vsl] = on * _silu(g_ref[pl.ds(r0, C), vsl])
            return carry

        steps.append((state_step, out_step))

    def all_state(t, carry):
        for state_step, _ in steps:
            state_step(t, 0)
        return carry

    def all_out(c, carry):
        for _, out_step in steps:
            out_step(c, 0)
        return carry

    if n_chunks <= 2:
        for t in range(n_chunks):
            all_state(t, 0)
        for c in range(n_chunks):
            all_out(c, 0)
    else:
        lax.fori_loop(0, n_chunks, all_state, 0)
        lax.fori_loop(0, n_chunks, all_out, 0, unroll=2)
    for hh in range(HB):
        s_ref[0, hh] = sf_run[hh]
        s_ref[1, hh] = sb_run[hh]


def _retention(pm3, rd4, ropes, s0, layer=0):
    B, L, _ = pm3.shape
    n_chunks = L // CHUNK
    has_ctx = s0 is not None
    H = RET_HEADS
    HB = 2 if n_chunks > 2 else H
    kw, vw = HB * RET_DK, HB * RET_DV
    in_specs = [
        pl.BlockSpec((None, L, kw), lambda b, h: (b, 0, OFF_RQ // kw + h)),
        pl.BlockSpec((None, L, kw), lambda b, h: (b, 0, OFF_RK // kw + h)),
        pl.BlockSpec((None, L, vw), lambda b, h: (b, 0, OFF_RV // vw + h)),
        pl.BlockSpec((None, L, vw), lambda b, h: (b, 0, OFF_RG // vw + h)),
        pl.BlockSpec((2, HB, 1, RET_DV), lambda b, h: (0, h, 0, 0)),
    ]
    args = [pm3, pm3, pm3, pm3, rd4]
    if has_ctx:
        in_specs += [
            pl.BlockSpec((L, 128), lambda b, h: (0, 0)),
            pl.BlockSpec((L, 128), lambda b, h: (0, 0)),
            pl.BlockSpec((None, None, 2, HB, RET_DK, RET_DV), lambda b, h: (b, layer, 0, h, 0, 0)),
        ]
        args += [ropes[0], ropes[1], s0]
    return pl.pallas_call(
        functools.partial(_ret_kernel, n_chunks=n_chunks, has_ctx=has_ctx, HB=HB),
        grid=(B, H // HB),
        in_specs=in_specs,
        out_specs=[
            pl.BlockSpec((None, L, vw), lambda b, h: (b, 0, h)),
            pl.BlockSpec((None, 2, HB, RET_DK, RET_DV), lambda b, h: (b, 0, h, 0, 0)),
        ],
        out_shape=[
            jax.ShapeDtypeStruct((B, L, D), F32),
            jax.ShapeDtypeStruct((B, 2, H, RET_DK, RET_DV), F32),
        ],
        scratch_shapes=[
            pltpu.VMEM((HB, L, RET_DK), F32),
            pltpu.VMEM((HB, L, RET_DK), F32),
            pltpu.VMEM((HB, n_chunks, 2 * RET_DK, RET_DV), BF16),
            pltpu.VMEM((HB, RET_DK, RET_DV), F32),
            pltpu.VMEM((HB, RET_DK, RET_DV), F32),
        ],
        compiler_params=_cparams(("arbitrary", "arbitrary")),
    )(*args)


def _lru_kernel(*refs, L, has_ctx):
    if has_ctx:
        (x_ref, gt_ref, cw_ref, cb_ref, w_ref, b_ref, lam_ref, h0_ref,
         o_ref, s_ref, xc_scr, hf_scr, pb_scr, ub_scr) = refs
    else:
        (x_ref, gt_ref, cw_ref, cb_ref, w_ref, b_ref, lam_ref,
         o_ref, s_ref, xc_scr, hf_scr, pb_scr, ub_scr) = refs
    G = LRU_GROUP
    NG = LRU_ITER_GROUPS
    n_it = L // (NG * G)
    W = LRU_BS
    BB = x_ref.shape[0]

    row = lax.broadcasted_iota(jnp.int32, (L, W), 0)
    for bb in range(BB):
        x = x_ref[bb]
        xm1 = jnp.where(row >= 1, pltpu.roll(x, 1, axis=0), 0.0)
        xp1 = jnp.where(row < L - 1, pltpu.roll(x, L - 1, axis=0), 0.0)
        xp2 = jnp.where(row < L - 2, pltpu.roll(x, L - 2, axis=0), 0.0)
        xc_scr[bb] = (cw_ref[0:1, :] * xm1 + cw_ref[1:2, :] * x + cw_ref[2:3, :] * xp1
                      + cw_ref[3:4, :] * xp2 + cb_ref[...])

    nsp_f = -LRU_C * _softplus(-lam_ref[0:1, :])
    nsp_b = -LRU_C * _softplus(-lam_ref[1:2, :])
    row8 = lax.broadcasted_iota(jnp.int32, (8, W), 0)

    def gates(zr, zi, nsp, xc):
        log_a = _sigmoid(zr) * nsp
        a = jnp.exp(log_a)
        u = jnp.sqrt(-jnp.tanh(log_a) * (a * a + 1.0)) * (_sigmoid(zi) * xc)
        return a, u

    def seg_scan(p, hl, carry, reverse):
        a, u = p, hl
        for s in (1, 2, 4):
            sh = (8 - s) if reverse else s
            a_sh = pltpu.roll(a, sh, axis=0)
            u_sh = pltpu.roll(u, sh, axis=0)
            m = (row8 < 8 - s) if reverse else (row8 >= s)
            u = jnp.where(m, a * u_sh + u, u)
            a = jnp.where(m, a * a_sh, a)
        h_end = u + a * carry
        if reverse:
            h_in = jnp.where(row8 == 7, carry, pltpu.roll(h_end, 7, axis=0))
            return h_in, h_end[0:1]
        h_in = jnp.where(row8 == 0, carry, pltpu.roll(h_end, 1, axis=0))
        return h_in, h_end[7:8]

    def pass1(it, carries):
        return tuple(pass1_one(it, carries[bb], xc_scr.at[bb], hf_scr.at[bb], pb_scr.at[bb], ub_scr.at[bb])
                     for bb in range(BB))

    def pass2(t, carries):
        return tuple(pass2_one(t, carries[bb], gt_ref.at[bb], o_ref.at[bb], hf_scr.at[bb], pb_scr.at[bb],
                               ub_scr.at[bb]) for bb in range(BB))

    def pass1_one(it, carry, xc_scr, hf_scr, pb_scr, ub_scr):
        base = pl.multiple_of(it * (NG * G), NG * G)
        xg = jnp.concatenate([xc_scr[pl.ds(base + gi * G + r, 8, stride=8), :]
                              for gi in range(NG) for r in range(8)], axis=0)
        zz = _dot(xg.astype(BF16), w_ref[...]) + b_ref[...]
        a_f, u_f = gates(zz[:, 0:W], zz[:, W:2 * W], nsp_f, xg)
        a_b, u_b = gates(zz[:, 2 * W:3 * W], zz[:, 3 * W:4 * W], nsp_b, xg)
        for gi in range(NG):
            v = lambda arr, r: arr[gi * G + r * 8:gi * G + (r + 1) * 8]
            p = [v(a_f, 0)]
            hl = [v(u_f, 0)]
            for r in range(1, 8):
                hl.append(v(a_f, r) * hl[-1] + v(u_f, r))
                p.append(v(a_f, r) * p[-1])
            h_in, carry = seg_scan(p[7], hl[7], carry, False)
            hf_scr[pl.ds(base + gi * G, G), :] = jnp.concatenate(
                [hl[r] + p[r] * h_in for r in range(8)], axis=0)
            pb = [None] * 8
            hb = [None] * 8
            pb[7] = v(a_b, 7)
            hb[7] = v(u_b, 7)
            for r in range(6, -1, -1):
                hb[r] = v(a_b, r) * hb[r + 1] + v(u_b, r)
                pb[r] = v(a_b, r) * pb[r + 1]
            pb_scr[pl.ds(base + gi * G, G), :] = jnp.concatenate(pb, axis=0)
            ub_scr[pl.ds(base + gi * G, G), :] = jnp.concatenate(hb, axis=0)
        return carry

    def pass2_one(t, carry, gt_ref, o_ref, hf_scr, pb_scr, ub_scr):
        base = pl.multiple_of((n_it - 1 - t) * (NG * G), NG * G)
        for gi in range(NG - 1, -1, -1):
            gb = base + gi * G
            pb = pb_scr[pl.ds(gb, G), :]
            hb = ub_scr[pl.ds(gb, G), :]
            hf = hf_scr[pl.ds(gb, G), :]
            h_in, carry = seg_scan(pb[0:8], hb[0:8], carry, True)
            for r in range(8):
                sl = slice(r * 8, (r + 1) * 8)
                gt = gt_ref[pl.ds(gb + r, 8, stride=8), :]
                o_ref[pl.ds(gb + r, 8, stride=8), :] = (hf[sl] + hb[sl] + pb[sl] * h_in) * _silu(gt)
        return carry

    if has_ctx:
        h0f = tuple(h0_ref[bb, 0:1, :] for bb in range(BB))
        h0b = tuple(h0_ref[bb, 1:2, :] for bb in range(BB))
    else:
        h0f = h0b = tuple(jnp.zeros((1, W), F32) for _ in range(BB))
    if n_it == 1:
        lf = pass1(0, h0f)
        lb = pass2(0, h0b)
    else:
        lf = lax.fori_loop(0, n_it, pass1, h0f)
        lb = lax.fori_loop(0, n_it, pass2, h0b)
    for bb in range(BB):
        s_ref[bb, 0:1, :] = lf[bb]
        s_ref[bb, 1:2, :] = lb[bb]


def _rglru(pm3, conv_w, conv_b, w_gate, b_gate, lam, h0, layer=0):
    B, L, _ = pm3.shape
    has_ctx = h0 is not None
    W = LRU_BS
    BB = LRU_BATCH_SHORT if L <= LRU_ITER_GROUPS * LRU_GROUP else LRU_BATCH_LONG
    in_specs = [
        pl.BlockSpec((BB, L, W), lambda b, n: (b, 0, OFF_LX // W + n)),
        pl.BlockSpec((BB, L, W), lambda b, n: (b, 0, OFF_LG // W + n)),
        pl.BlockSpec((4, W), lambda b, n: (0, n)),
        pl.BlockSpec((1, W), lambda b, n: (0, n)),
        pl.BlockSpec((None, W, 4 * W), lambda b, n: (n, 0, 0)),
        pl.BlockSpec((None, 1, 4 * W), lambda b, n: (n, 0, 0)),
        pl.BlockSpec((2, W), lambda b, n: (0, n)),
    ]
    args = [pm3, pm3, conv_w, conv_b, w_gate, b_gate, lam]
    if has_ctx:
        in_specs.append(pl.BlockSpec((BB, None, 2, W), lambda b, n: (b, layer, 0, n)))
        args.append(h0)
    return pl.pallas_call(
        functools.partial(_lru_kernel, L=L, has_ctx=has_ctx),
        grid=(B // BB, LRU_BLOCKS),
        in_specs=in_specs,
        out_specs=[
            pl.BlockSpec((BB, L, W), lambda b, n: (b, 0, n)),
            pl.BlockSpec((BB, 2, W), lambda b, n: (b, 0, n)),
        ],
        out_shape=[
            jax.ShapeDtypeStruct((B, L, D), F32),
            jax.ShapeDtypeStruct((B, 2, D), F32),
        ],
        scratch_shapes=[pltpu.VMEM((BB, L, W), F32)] * 4,
        compiler_params=_cparams(("arbitrary", "arbitrary")),
    )(*args)


def _att_kernel(*refs, L, Q, NB, has_ctx):
    if NB == 1:
        _att_one(*refs, L=L, Q=Q, has_ctx=has_ctx)
    else:
        for bb in range(NB):
            _att_one(refs[0], *[r.at[bb] for r in refs[1:]], L=L, Q=Q, has_ctx=has_ctx)


def _att_one(*refs, L, Q, has_ctx):
    if has_ctx:
        (sink_ref, q_ref, ag_ref, k_ref, v_ref, ck_ref, cv_ref, cosq_ref, sinq_ref, cosk_ref, sink_tab_ref,
         o_ref, km, vm, ckm, cvm) = refs
    else:
        (sink_ref, q_ref, ag_ref, k_ref, v_ref, o_ref, km, vm) = refs
    G = ATT_GROUP
    KW = ATT_KV_W
    qi = pl.program_id(1)

    def head_masked(dst, val):
        head = lax.shift_right_logical(lax.broadcasted_iota(jnp.int32, val.shape, 1), 6)
        for h in range(ATT_KV_HEADS):
            dst[h] = jnp.where(head == h, val, 0.0).astype(BF16)

    @pl.when(qi == 0)
    def _():
        kk = k_ref[...]
        if has_ctx:
            kk = _rope(kk, jnp.concatenate([cosk_ref[...]] * 2, axis=1),
                       jnp.concatenate([sink_tab_ref[...]] * 2, axis=1), 16)
            head_masked(ckm, ck_ref[...])
            head_masked(cvm, cv_ref[...])
        head_masked(km, kk)
        head_masked(vm, v_ref[...])

    q = q_ref[...] * (ATT_HD ** -0.5)
    if has_ctx:
        q = _rope(q, jnp.concatenate([cosq_ref[...]] * 8, axis=1),
                  jnp.concatenate([sinq_ref[...]] * 8, axis=1), 16)
        span = Q + 2 * WINDOW
        start = pl.multiple_of(jnp.clip((qi - 1) * Q, 0, L - span), Q)
        qpos = qi * Q + lax.broadcasted_iota(jnp.int32, (Q, span), 0)
        kpos = start + lax.broadcasted_iota(jnp.int32, (Q, span), 1)
        band = jnp.abs(qpos - kpos) <= WINDOW
        band4 = jnp.concatenate([band] * G, axis=0)
    qg = jnp.concatenate([q[:, g * KW:(g + 1) * KW] for g in range(G)], axis=0).astype(BF16)

    o = None
    for h in range(ATT_KV_HEADS):
        sk = jnp.concatenate(
            [jnp.full((Q, 1), sink_ref[h * G + g], F32) for g in range(G)], axis=0)
        if has_ctx:
            s = jnp.concatenate([jnp.where(band4, _dot_nt(qg, km[h, pl.ds(start, span), :]), -jnp.inf),
                                 _dot_nt(qg, ckm[h])], axis=1)
            m = jnp.maximum(jnp.max(s, axis=-1, keepdims=True), sk)
            p = jnp.exp(s - m)
            den = jnp.sum(p, axis=-1, keepdims=True) + jnp.exp(sk - m)
            pb = p.astype(BF16)
            oh = _dot(pb[:, :span], vm[h, pl.ds(start, span), :]) + _dot(pb[:, span:], cvm[h])
        else:
            s1 = _dot_nt(qg, km[h])
            m = jnp.maximum(jnp.max(s1, axis=-1, keepdims=True), sk)
            p1 = jnp.exp(s1 - m)
            den = jnp.sum(p1, axis=-1, keepdims=True) + jnp.exp(sk - m)
            oh = _dot(p1.astype(BF16), vm[h])
        oh = oh * (1.0 / den)
        o = oh if o is None else o + oh
    ao = jnp.concatenate([o[g * Q:(g + 1) * Q] for g in range(G)], axis=1)
    o_ref[...] = ao * _silu(ag_ref[...])


def _attention(pm3, sink, ropes, ck, cv, layer=0):
    B, L, _ = pm3.shape
    has_ctx = ck is not None
    Q = CHUNK if has_ctx else L
    NB = 1 if has_ctx else 2
    nb = None if NB == 1 else NB
    lead = () if NB == 1 else (NB,)
    KW = ATT_KV_W
    in_specs = [
        pl.BlockSpec(memory_space=pltpu.SMEM),
        pl.BlockSpec((nb, Q, D), lambda b, i: (b, i, OFF_AQ // D)),
        pl.BlockSpec((nb, Q, D), lambda b, i: (b, i, OFF_AG // D)),
        pl.BlockSpec((nb, L, KW), lambda b, i: (b, 0, OFF_AK // KW)),
        pl.BlockSpec((nb, L, KW), lambda b, i: (b, 0, OFF_AV // KW)),
    ]
    args = [sink, pm3, pm3, pm3, pm3]
    scratch = [pltpu.VMEM(lead + (ATT_KV_HEADS, L, KW), BF16), pltpu.VMEM(lead + (ATT_KV_HEADS, L, KW), BF16)]
    if has_ctx:
        P = ck.shape[2]
        in_specs += [
            pl.BlockSpec((None, None, P, KW), lambda b, i: (b, layer, 0, 0)),
            pl.BlockSpec((None, None, P, KW), lambda b, i: (b, layer, 0, 0)),
            pl.BlockSpec((Q, 128), lambda b, i: (i, 0)),
            pl.BlockSpec((Q, 128), lambda b, i: (i, 0)),
            pl.BlockSpec((L, 128), lambda b, i: (0, 0)),
            pl.BlockSpec((L, 128), lambda b, i: (0, 0)),
        ]
        args += [ck, cv, ropes[0], ropes[1], ropes[0], ropes[1]]
        scratch += [pltpu.VMEM((ATT_KV_HEADS, P, KW), BF16), pltpu.VMEM((ATT_KV_HEADS, P, KW), BF16)]
    return pl.pallas_call(
        functools.partial(_att_kernel, L=L, Q=Q, NB=NB, has_ctx=has_ctx),
        grid=(B // NB, L // Q),
        in_specs=in_specs,
        out_specs=pl.BlockSpec((nb, Q, D), lambda b, i: (b, i, 0)),
        out_shape=jax.ShapeDtypeStruct((B, L, D), F32),
        scratch_shapes=scratch,
        compiler_params=_cparams(("arbitrary", "arbitrary")),
    )(*args)


OUT_TM = 512


def _out_kernel(x_ref, ro_ref, lo_ref, ao_ref, mod_ref, gpre_ref, gpost_ref,
                wm_ref, bm_ref, wb_ref, wo_ref, y_ref):
    x = x_ref[...]
    mod = mod_ref[...]
    h = _modulated_norm(x, gpre_ref[...], mod).astype(BF16)
    merged = None
    for n, br_ref in enumerate((ro_ref, lo_ref, ao_ref)):
        gate = _sigmoid(_dot(h, wm_ref[:, n * D:(n + 1) * D]) + bm_ref[:, n * D:(n + 1) * D])
        z = gate * _dot(br_ref[...].astype(BF16), wb_ref[n])
        merged = z if merged is None else merged + z
    out = _dot(merged.astype(BF16), wo_ref[...])
    nrm = out * lax.rsqrt(jnp.mean(out * out, axis=-1, keepdims=True) + EPS) * gpost_ref[...]
    y_ref[...] = x + mod[:, 2 * D:3 * D] * nrm


def _merge_out(x2d, ro, lo, ao, mod3, row_fn, g_pre, g_post, w_merge, b_merge, w_branch, w_out):
    T = x2d.shape[0]
    tm = OUT_TM
    tok = lambda i: (i, 0)
    once = pl.Buffered(1)
    return pl.pallas_call(
        _out_kernel,
        grid=(T // tm,),
        in_specs=[
            pl.BlockSpec((tm, D), tok),
            pl.BlockSpec((tm, D), tok),
            pl.BlockSpec((tm, D), tok),
            pl.BlockSpec((tm, D), tok),
            pl.BlockSpec((None, 1, 3 * D), lambda i: (row_fn(i * tm), 0, 0)),
            pl.BlockSpec((1, D), lambda i: (0, 0)),
            pl.BlockSpec((1, D), lambda i: (0, 0)),
            pl.BlockSpec((D, N_BRANCH * D), lambda i: (0, 0), pipeline_mode=once),
            pl.BlockSpec((1, N_BRANCH * D), lambda i: (0, 0)),
            pl.BlockSpec((N_BRANCH, D, D), lambda i: (0, 0, 0), pipeline_mode=once),
            pl.BlockSpec((D, D), lambda i: (0, 0), pipeline_mode=once),
        ],
        out_specs=pl.BlockSpec((tm, D), tok),
        out_shape=jax.ShapeDtypeStruct((T, D), F32),
        compiler_params=_cparams(("arbitrary",)),
    )(x2d, ro, lo, ao, mod3, g_pre, g_post, w_merge, b_merge, w_branch, w_out)


def _layer(x, mod3, row_of_token, lw, ctx, ropes_ret, ropes_att):
    B, L, _ = x.shape
    x2d = x.reshape(B * L, D)
    pm2 = _in_proj(x2d, mod3, row_of_token, lw['g_pre'], lw['w_in'])
    pm3 = pm2.reshape(B, L, W_IN)
    if ctx is None:
        ro, ret_s = _retention(pm3, lw['rd4'], None, None)
        lo, lru_s = _rglru(pm3, lw['conv_w'], lw['conv_b'], lw['w_gate'], lw['b_gate'], lw['lam'], None)
        ao = _attention(pm3, lw['sink'], None, None, None)
    else:
        s_ret, s_lru, ck, cv, layer = ctx
        ro, ret_s = _retention(pm3, lw['rd4'], ropes_ret, s_ret, layer)
        lo, lru_s = _rglru(pm3, lw['conv_w'], lw['conv_b'], lw['w_gate'], lw['b_gate'], lw['lam'], s_lru, layer)
        ao = _attention(pm3, lw['sink'], ropes_att, ck, cv, layer)
    y = _merge_out(x2d, ro.reshape(B * L, D), lo.reshape(B * L, D), ao.reshape(B * L, D),
                   mod3, row_of_token, lw['g_pre'], lw['g_post'], lw['w_merge'], lw['b_merge'],
                   lw['w_branch'], lw['w_out'])
    return y.reshape(B, L, D), pm3, ret_s, lru_s


def _group_major(w, axis):
    shp = w.shape
    w = w.reshape(shp[:axis] + (ATT_KV_HEADS, ATT_GROUP, ATT_HD) + shp[axis + 1:])
    w = jnp.swapaxes(w, axis, axis + 1)
    return w.reshape(shp)


def _layer_weights(l, norm_pre, norm_post, w_in, ret_decay, conv_w, conv_b, lru_wa, lru_ba, lru_wx,
                   lru_bx, lru_lambda, att_sink, w_branch, w_merge, b_merge, w_out):
    offs = np.cumsum((512, 512, 1024, 1024, 1024, 1024, 1024, 256, 256, 1024))[:-1].tolist()
    rq, rk, rv, rg, lx, lgt, aq, ak, av, ag = jnp.split(w_in[l], offs, axis=-1)
    w_cat = jnp.concatenate([_group_major(aq, 1), _group_major(ag, 1), rv, rg, lx, lgt, rq, rk, ak, av],
                            axis=-1).astype(BF16)
    wb = jnp.stack([w_branch[l, 0], w_branch[l, 1], _group_major(w_branch[l, 2], 0)], axis=0).astype(BF16)
    w_gate = jnp.concatenate([lru_wa[l, 0], lru_wx[l, 0], lru_wa[l, 1], lru_wx[l, 1]], axis=-1).astype(BF16)
    b4 = jnp.stack([lru_ba[l, 0], lru_bx[l, 0], lru_ba[l, 1], lru_bx[l, 1]], axis=0)
    b_gate = b4.reshape(4, LRU_BLOCKS, LRU_BS).transpose(1, 0, 2).reshape(LRU_BLOCKS, 1, 4 * LRU_BS)
    rd4 = jnp.broadcast_to(ret_decay[l][:, :, None, None], (2, RET_HEADS, 1, RET_DV))
    return {
        'g_pre': norm_pre[l].reshape(1, D), 'g_post': norm_post[l].reshape(1, D),
        'w_in': w_cat, 'rd4': rd4,
        'conv_w': conv_w[l], 'conv_b': conv_b[l].reshape(1, D),
        'w_gate': w_gate, 'b_gate': b_gate, 'lam': lru_lambda[l],
        'sink': att_sink[l], 'w_merge': w_merge[l].astype(BF16), 'b_merge': b_merge[l].reshape(1, N_BRANCH * D),
        'w_branch': wb, 'w_out': w_out[l].astype(BF16),
    }


def kernel(x_prompt, x_sample, cache_k, cache_v, state_ret, state_lru, c, c_ctx, w_ada, b_ada, norm_pre, norm_post, w_in, ret_decay, conv_w, conv_b, lru_wa, lru_ba, lru_wx, lru_bx, lru_lambda, att_sink, w_branch, w_merge, b_merge, w_out):
    B, S, _ = x_prompt.shape
    Bd, Ld, _ = x_sample.shape
    P = cache_k.shape[2]
    cond8 = jnp.concatenate([c_ctx[None, :], c, jnp.zeros((8 - 1 - Bd, D), F32)], axis=0)
    mod = _ada_mod(cond8, w_ada, b_ada)
    ropes_ret = _rope_tables(Ld, RET_DK)
    ropes_att = _rope_tables(Ld, ATT_HD)

    yp, ys = x_prompt, x_sample
    new_k, new_v, new_ret, new_lru = [], [], [], []
    for l in range(DEPTH):
        lw = _layer_weights(l, norm_pre, norm_post, w_in, ret_decay, conv_w, conv_b, lru_wa, lru_ba,
                            lru_wx, lru_bx, lru_lambda, att_sink, w_branch, w_merge, b_merge, w_out)
        mod3 = mod[l].reshape(8, 1, 3 * D)
        yp, pm_p, r_s, l_s = _layer(yp, mod3, lambda t: 0, lw, None, None, None)
        new_ret.append(r_s)
        new_lru.append(l_s)
        new_k.append(pm_p[:, :, OFF_AK:OFF_AK + ATT_KV_W].reshape(B, S, ATT_KV_HEADS, ATT_HD))
        new_v.append(pm_p[:, :, OFF_AV:OFF_AV + ATT_KV_W].reshape(B, S, ATT_KV_HEADS, ATT_HD))
        ctx = (state_ret, state_lru, cache_k.reshape(Bd, DEPTH, P, ATT_KV_W),
               cache_v.reshape(Bd, DEPTH, P, ATT_KV_W), l)
        ys, _, _, _ = _layer(ys, mod3, lambda t: 1 + t // Ld, lw, ctx, ropes_ret, ropes_att)
    return (yp, ys, jnp.stack(new_k, axis=1), jnp.stack(new_v, axis=1),
            jnp.stack(new_ret, axis=1), jnp.stack(new_lru, axis=1))
```

```python
import functools

import numpy as np
import jax
import jax.numpy as jnp
from jax import lax
from jax.experimental import pallas as pl
from jax.experimental.pallas import tpu as pltpu

F32 = jnp.float32
BF16 = jnp.bfloat16

D = 1024
DEPTH = 2
GRID_W = 64
EPS = 1e-6
N_BRANCH = 3
RET_HEADS = 4
RET_DV = 256
RET_DK = 128
CHUNK = 128
LRU_BLOCKS = 8
LRU_BS = 128
LRU_C = 8.0
LRU_GROUP = 64
LRU_ITER_GROUPS = 4
LRU_BATCH_LONG = 2
LRU_BATCH_SHORT = 4
ATT_HD = 64
ATT_Q_HEADS = 16
ATT_KV_HEADS = 4
ATT_GROUP = 4
ATT_KV_W = 256
WINDOW = 128
ROPE_BASE = 10000.0

OFF_AQ = 0
OFF_AG = 1024
OFF_RV = 2048
OFF_RG = 3072
OFF_LX = 4096
OFF_LG = 5120
OFF_RQ = 6144
OFF_RK = 6656
OFF_AK = 7168
OFF_AV = 7424
W_IN = 7680

VMEM_LIMIT = 56 * 1024 * 1024


def _cparams(sem):
    return pltpu.CompilerParams(dimension_semantics=sem, vmem_limit_bytes=VMEM_LIMIT)


def _sigmoid(x):
    return 0.5 * jnp.tanh(0.5 * x) + 0.5


def _silu(x):
    return x * _sigmoid(x)


def _softplus(z):
    return jnp.maximum(z, 0.0) + jnp.log1p(jnp.exp(-jnp.abs(z)))


def _dot(a, b):
    return jnp.dot(a, b, preferred_element_type=F32)


def _dot_nt(a, b):
    return lax.dot_general(a, b, (((1,), (1,)), ((), ())), preferred_element_type=F32)


def _dot_tn(a, b):
    return lax.dot_general(a, b, (((0,), (0,)), ((), ())), preferred_element_type=F32)


def _modulated_norm(x, g, mod):
    y = x * lax.rsqrt(jnp.mean(x * x, axis=-1, keepdims=True) + EPS) * g
    return y * (1.0 + mod[:, D:2 * D]) + mod[:, 0:D]


def _ada_kernel(c_ref, w_ref, b_ref, o_ref):
    s = _silu(c_ref[...])
    o_ref[...] = _dot(s.astype(BF16), w_ref[...].astype(BF16)) + b_ref[...]


def _ada_mod(cond8, w_ada, b_ada):
    tn = 1024
    return pl.pallas_call(
        _ada_kernel,
        grid=(DEPTH, 3 * D // tn),
        in_specs=[
            pl.BlockSpec((8, D), lambda l, j: (0, 0)),
            pl.BlockSpec((None, D, tn), lambda l, j: (l, 0, j)),
            pl.BlockSpec((None, 1, tn), lambda l, j: (l, 0, j)),
        ],
        out_specs=pl.BlockSpec((None, 8, tn), lambda l, j: (l, 0, j)),
        out_shape=jax.ShapeDtypeStruct((DEPTH, 8, 3 * D), F32),
        compiler_params=_cparams(("arbitrary", "arbitrary")),
    )(cond8, w_ada, b_ada.reshape(DEPTH, 1, 3 * D))


IN_TM = 1024
IN_TN = 1280


def _in_kernel(x_ref, mod_ref, g_ref, w_ref, o_ref, h_all):
    j = pl.program_id(0)
    i = pl.program_id(1)
    r0 = pl.multiple_of(i * IN_TM, IN_TM)

    @pl.when(j == 0)
    def _():
        h = _modulated_norm(x_ref[...], g_ref[...], mod_ref[...])
        h_all[pl.ds(r0, IN_TM), :] = h.astype(BF16)

    o_ref[...] = _dot(h_all[pl.ds(r0, IN_TM), :], w_ref[...])


def _in_proj(x2d, mod3, row_fn, g_pre, w_in):
    T = x2d.shape[0]
    n_m = T // IN_TM
    return pl.pallas_call(
        _in_kernel,
        grid=(W_IN // IN_TN, n_m),
        in_specs=[
            pl.BlockSpec((IN_TM, D), lambda j, i: (jnp.where(j == 0, i, n_m - 1), 0)),
            pl.BlockSpec((None, 1, 3 * D), lambda j, i: (row_fn(i * IN_TM), 0, 0)),
            pl.BlockSpec((1, D), lambda j, i: (0, 0)),
            pl.BlockSpec((D, IN_TN), lambda j, i: (0, j)),
        ],
        out_specs=pl.BlockSpec((IN_TM, IN_TN), lambda j, i: (i, j)),
        out_shape=jax.ShapeDtypeStruct((T, W_IN), F32),
        scratch_shapes=[pltpu.VMEM((T, D), BF16)],
        compiler_params=_cparams(("arbitrary", "arbitrary")),
    )(x2d, mod3, g_pre, w_in)


def _rope(x, cos_t, sin_t, half):
    lane = lax.broadcasted_iota(jnp.int32, x.shape, 1)
    up = pltpu.roll(x, x.shape[1] - half, axis=1)
    dn = pltpu.roll(x, half, axis=1)
    partner = jnp.where((lane & half) == 0, up, dn)
    return x * cos_t + partner * sin_t


def _rope_tables(n_tokens, dim):
    nq = dim // 4
    lane = np.arange(128)
    within = lane % dim
    axis = within // (2 * nq)
    freq = within % nq
    sign = np.where((within % (2 * nq)) < nq, -1.0, 1.0).astype(np.float32)
    inv = ROPE_BASE ** (-jnp.arange(nq, dtype=F32) / nq)
    t = jnp.arange(n_tokens)
    pos = jnp.stack([(t // GRID_W).astype(F32), (t % GRID_W).astype(F32)], axis=1)
    ang = pos[:, axis] * inv[freq][None, :]
    return jnp.cos(ang), jnp.sin(ang) * sign[None, :]


def _ret_kernel(*refs, n_chunks, has_ctx, HB):
    if has_ctx:
        (q_ref, k_ref, v_ref, g_ref, rd_ref, cos_ref, sin_ref, s0_ref,
         o_ref, s_ref, qs, ks, s_all, sf_run, sb_run) = refs
    else:
        (q_ref, k_ref, v_ref, g_ref, rd_ref,
         o_ref, s_ref, qs, ks, s_all, sf_run, sb_run) = refs
    C = CHUNK
    ii = lax.broadcasted_iota(jnp.int32, (C, C), 0)
    jj = lax.broadcasted_iota(jnp.int32, (C, C), 1)
    diff = (ii - jj).astype(F32)
    ri = ii.astype(F32)

    steps = []
    for hh in range(HB):
        ksl = slice(hh * RET_DK, (hh + 1) * RET_DK)
        vsl = slice(hh * RET_DV, (hh + 1) * RET_DV)

        q = q_ref[:, ksl] * (RET_DK ** -0.5)
        k = k_ref[:, ksl]
        if has_ctx:
            q = _rope(q, cos_ref[...], sin_ref[...], 32)
            k = _rope(k, cos_ref[...], sin_ref[...], 32)
            sf_run[hh] = s0_ref[0, hh]
            sb_run[hh] = s0_ref[1, hh]
        else:
            sf_run[hh] = jnp.zeros((RET_DK, RET_DV), F32)
            sb_run[hh] = jnp.zeros((RET_DK, RET_DV), F32)
        qs[hh] = q
        ks[hh] = k

        lg_f = -_softplus(-rd_ref[0, hh])
        lg_b = -_softplus(-rd_ref[1, hh])
        lgf = lg_f[:, :C]
        lgb = lg_b[:, :C]
        dmat = (jnp.where(diff >= 0, jnp.exp(lgf * jnp.maximum(diff, 0.0)), 0.0)
                + jnp.where(diff <= 0, jnp.exp(lgb * jnp.maximum(-diff, 0.0)), 0.0))
        dq_f = jnp.exp(lgf * (ri + 1.0))
        dk_f = jnp.exp(lgf * (C - 1.0 - ri))
        dq_b = jnp.exp(lgb * (C - ri))
        dk_b = jnp.exp(lgb * ri)
        dc_f = jnp.exp(lg_f * float(C))
        dc_b = jnp.exp(lg_b * float(C))

        def state_step(t, carry, hh=hh, vsl=vsl, dk_f=dk_f, dk_b=dk_b, dc_f=dc_f, dc_b=dc_b):
            cf = t
            cb = n_chunks - 1 - t
            rf = pl.multiple_of(cf * C, C)
            rb = pl.multiple_of(cb * C, C)
            sf = sf_run[hh]
            sb = sb_run[hh]
            s_all[hh, cf, 0:RET_DK, :] = sf.astype(BF16)
            s_all[hh, cb, RET_DK:2 * RET_DK, :] = sb.astype(BF16)
            kf = (ks[hh, pl.ds(rf, C), :] * dk_f).astype(BF16)
            kb = (ks[hh, pl.ds(rb, C), :] * dk_b).astype(BF16)
            sf_run[hh] = sf * dc_f + _dot_tn(kf, v_ref[pl.ds(rf, C), vsl].astype(BF16))
            sb_run[hh] = sb * dc_b + _dot_tn(kb, v_ref[pl.ds(rb, C), vsl].astype(BF16))
            return carry

        def out_step(c, carry, hh=hh, vsl=vsl, dmat=dmat, dq_f=dq_f, dq_b=dq_b):
            r0 = pl.multiple_of(c * C, C)
            qc = qs[hh, pl.ds(r0, C), :]
            kc = ks[hh, pl.ds(r0, C), :]
            vc = v_ref[pl.ds(r0, C), vsl].astype(BF16)
            a = _dot_nt(qc.astype(BF16), kc.astype(BF16)) * dmat
            qq = jnp.concatenate([qc * dq_f, qc * dq_b], axis=1).astype(BF16)
            o = _dot(a.astype(BF16), vc) + _dot(qq, s_all[hh, c])
            mu = jnp.mean(o, axis=-1, keepdims=True)
            oc = o - mu
            on = oc * lax.rsqrt(jnp.mean(oc * oc, axis=-1, keepdims=True) + EPS)
            o_ref[pl.ds(r0, C), vsl] = on * _silu(g_ref[pl.ds(r0, C), vsl])
            return carry

        steps.append((state_step, out_step))

    def all_state(t, carry):
        for state_step, _ in steps:
            state_step(t, 0)
        return carry

    def all_out(c, carry):
        for _, out_step in steps:
            out_step(c, 0)
        return carry

    if n_chunks <= 2:
        for t in range(n_chunks):
            all_state(t, 0)
        for c in range(n_chunks):
            all_out(c, 0)
    else:
        lax.fori_loop(0, n_chunks, all_state, 0)
        lax.fori_loop(0, n_chunks, all_out, 0, unroll=2)
    for hh in range(HB):
        s_ref[0, hh] = sf_run[hh]
        s_ref[1, hh] = sb_run[hh]


def _retention(pm3, rd4, ropes, s0, layer=0):
    B, L, _ = pm3.shape
    n_chunks = L // CHUNK
    has_ctx = s0 is not None
    H = RET_HEADS
    HB = 2 if n_chunks > 2 else H
    kw, vw = HB * RET_DK, HB * RET_DV
    in_specs = [
        pl.BlockSpec((None, L, kw), lambda b, h: (b, 0, OFF_RQ // kw + h)),
        pl.BlockSpec((None, L, kw), lambda b, h: (b, 0, OFF_RK // kw + h)),
        pl.BlockSpec((None, L, vw), lambda b, h: (b, 0, OFF_RV // vw + h)),
        pl.BlockSpec((None, L, vw), lambda b, h: (b, 0, OFF_RG // vw + h)),
        pl.BlockSpec((2, HB, 1, RET_DV), lambda b, h: (0, h, 0, 0)),
    ]
    args = [pm3, pm3, pm3, pm3, rd4]
    if has_ctx:
        in_specs += [
            pl.BlockSpec((L, 128), lambda b, h: (0, 0)),
            pl.BlockSpec((L, 128), lambda b, h: (0, 0)),
            pl.BlockSpec((None, None, 2, HB, RET_DK, RET_DV), lambda b, h: (b, layer, 0, h, 0, 0)),
        ]
        args += [ropes[0], ropes[1], s0]
    return pl.pallas_call(
        functools.partial(_ret_kernel, n_chunks=n_chunks, has_ctx=has_ctx, HB=HB),
        grid=(B, H // HB),
        in_specs=in_specs,
        out_specs=[
            pl.BlockSpec((None, L, vw), lambda b, h: (b, 0, h)),
            pl.BlockSpec((None, 2, HB, RET_DK, RET_DV), lambda b, h: (b, 0, h, 0, 0)),
        ],
        out_shape=[
            jax.ShapeDtypeStruct((B, L, D), F32),
            jax.ShapeDtypeStruct((B, 2, H, RET_DK, RET_DV), F32),
        ],
        scratch_shapes=[
            pltpu.VMEM((HB, L, RET_DK), F32),
            pltpu.VMEM((HB, L, RET_DK), F32),
            pltpu.VMEM((HB, n_chunks, 2 * RET_DK, RET_DV), BF16),
            pltpu.VMEM((HB, RET_DK, RET_DV), F32),
            pltpu.VMEM((HB, RET_DK, RET_DV), F32),
        ],
        compiler_params=_cparams(("arbitrary", "arbitrary")),
    )(*args)


def _lru_kernel(*refs, L, has_ctx):
    if has_ctx:
        (x_ref, gt_ref, cw_ref, cb_ref, w_ref, b_ref, lam_ref, h0_ref,
         o_ref, s_ref, xc_scr, hf_scr, pb_scr, ub_scr) = refs
    else:
        (x_ref, gt_ref, cw_ref, cb_ref, w_ref, b_ref, lam_ref,
         o_ref, s_ref, xc_scr, hf_scr, pb_scr, ub_scr) = refs
    G = LRU_GROUP
    NG = LRU_ITER_GROUPS
    n_it = L // (NG * G)
    W = LRU_BS
    BB = x_ref.shape[0]

    row = lax.broadcasted_iota(jnp.int32, (L, W), 0)
    for bb in range(BB):
        x = x_ref[bb]
        xm1 = jnp.where(row >= 1, pltpu.roll(x, 1, axis=0), 0.0)
        xp1 = jnp.where(row < L - 1, pltpu.roll(x, L - 1, axis=0), 0.0)
        xp2 = jnp.where(row < L - 2, pltpu.roll(x, L - 2, axis=0), 0.0)
        xc_scr[bb] = (cw_ref[0:1, :] * xm1 + cw_ref[1:2, :] * x + cw_ref[2:3, :] * xp1
                      + cw_ref[3:4, :] * xp2 + cb_ref[...])

    nsp_f = (-0.5 * LRU_C) * _softplus(-lam_ref[0:1, :])
    nsp_b = (-0.5 * LRU_C) * _softplus(-lam_ref[1:2, :])
    row8 = lax.broadcasted_iota(jnp.int32, (8, W), 0)

    def gates(zr, zi, nsp, xc):
        log_a = jnp.tanh(0.5 * zr) * nsp + nsp
        a = jnp.exp(log_a)
        u = jnp.sqrt(-jnp.tanh(log_a) * (a * a + 1.0)) * (_sigmoid(zi) * xc)
        return a, u

    def seg_scan(p, hl, carry, reverse):
        a, u = p, hl
        for s in (1, 2, 4):
            sh = (8 - s) if reverse else s
            a_sh = pltpu.roll(a, sh, axis=0)
            u_sh = pltpu.roll(u, sh, axis=0)
            m = (row8 < 8 - s) if reverse else (row8 >= s)
            u = jnp.where(m, a * u_sh + u, u)
            a = jnp.where(m, a * a_sh, a)
        h_end = u + a * carry
        if reverse:
            h_in = jnp.where(row8 == 7, carry, pltpu.roll(h_end, 7, axis=0))
            return h_in, h_end[0:1]
        h_in = jnp.where(row8 == 0, carry, pltpu.roll(h_end, 1, axis=0))
        return h_in, h_end[7:8]

    def pass1(it, carries):
        return tuple(pass1_one(it, carries[bb], xc_scr.at[bb], hf_scr.at[bb], pb_scr.at[bb], ub_scr.at[bb])
                     for bb in range(BB))

    def pass2(t, carries):
        return tuple(pass2_one(t, carries[bb], gt_ref.at[bb], o_ref.at[bb], hf_scr.at[bb], pb_scr.at[bb],
                               ub_scr.at[bb]) for bb in range(BB))

    def pass1_one(it, carry, xc_scr, hf_scr, pb_scr, ub_scr):
        base = pl.multiple_of(it * (NG * G), NG * G)
        xg = jnp.concatenate([xc_scr[pl.ds(base + gi * G + r, 8, stride=8), :]
                              for gi in range(NG) for r in range(8)], axis=0)
        zz = _dot(xg.astype(BF16), w_ref[...]) + b_ref[...]
        a_f, u_f = gates(zz[:, 0:W], zz[:, W:2 * W], nsp_f, xg)
        a_b, u_b = gates(zz[:, 2 * W:3 * W], zz[:, 3 * W:4 * W], nsp_b, xg)
        for gi in range(NG):
            v = lambda arr, r: arr[gi * G + r * 8:gi * G + (r + 1) * 8]
            p = [v(a_f, 0)]
            hl = [v(u_f, 0)]
            for r in range(1, 8):
                hl.append(v(a_f, r) * hl[-1] + v(u_f, r))
                p.append(v(a_f, r) * p[-1])
            h_in, carry = seg_scan(p[7], hl[7], carry, False)
            hf_scr[pl.ds(base + gi * G, G), :] = jnp.concatenate(
                [hl[r] + p[r] * h_in for r in range(8)], axis=0)
            pb = [None] * 8
            hb = [None] * 8
            pb[7] = v(a_b, 7)
            hb[7] = v(u_b, 7)
            for r in range(6, -1, -1):
                hb[r] = v(a_b, r) * hb[r + 1] + v(u_b, r)
                pb[r] = v(a_b, r) * pb[r + 1]
            pb_scr[pl.ds(base + gi * G, G), :] = jnp.concatenate(pb, axis=0)
            ub_scr[pl.ds(base + gi * G, G), :] = jnp.concatenate(hb, axis=0)
        return carry

    def pass2_one(t, carry, gt_ref, o_ref, hf_scr, pb_scr, ub_scr):
        base = pl.multiple_of((n_it - 1 - t) * (NG * G), NG * G)
        for gi in range(NG - 1, -1, -1):
            gb = base + gi * G
            pb = pb_scr[pl.ds(gb, G), :]
            hb = ub_scr[pl.ds(gb, G), :]
            hf = hf_scr[pl.ds(gb, G), :]
            h_in, carry = seg_scan(pb[0:8], hb[0:8], carry, True)
            for r in range(8):
                sl = slice(r * 8, (r + 1) * 8)
                gt = gt_ref[pl.ds(gb + r, 8, stride=8), :]
                o_ref[pl.ds(gb + r, 8, stride=8), :] = (hf[sl] + hb[sl] + pb[sl] * h_in) * _silu(gt)
        return carry

    if has_ctx:
        h0f = tuple(h0_ref[bb, 0:1, :] for bb in range(BB))
        h0b = tuple(h0_ref[bb, 1:2, :] for bb in range(BB))
    else:
        h0f = h0b = tuple(jnp.zeros((1, W), F32) for _ in range(BB))
    if n_it == 1:
        lf = pass1(0, h0f)
        lb = pass2(0, h0b)
    else:
        lf = lax.fori_loop(0, n_it, pass1, h0f)
        lb = lax.fori_loop(0, n_it, pass2, h0b)
    for bb in range(BB):
        s_ref[bb, 0:1, :] = lf[bb]
        s_ref[bb, 1:2, :] = lb[bb]


def _rglru(pm3, conv_w, conv_b, w_gate, b_gate, lam, h0, layer=0):
    B, L, _ = pm3.shape
    has_ctx = h0 is not None
    W = LRU_BS
    BB = LRU_BATCH_SHORT if L <= LRU_ITER_GROUPS * LRU_GROUP else LRU_BATCH_LONG
    in_specs = [
        pl.BlockSpec((BB, L, W), lambda b, n: (b, 0, OFF_LX // W + n)),
        pl.BlockSpec((BB, L, W), lambda b, n: (b, 0, OFF_LG // W + n)),
        pl.BlockSpec((4, W), lambda b, n: (0, n)),
        pl.BlockSpec((1, W), lambda b, n: (0, n)),
        pl.BlockSpec((None, W, 4 * W), lambda b, n: (n, 0, 0)),
        pl.BlockSpec((None, 1, 4 * W), lambda b, n: (n, 0, 0)),
        pl.BlockSpec((2, W), lambda b, n: (0, n)),
    ]
    args = [pm3, pm3, conv_w, conv_b, w_gate, b_gate, lam]
    if has_ctx:
        in_specs.append(pl.BlockSpec((BB, None, 2, W), lambda b, n: (b, layer, 0, n)))
        args.append(h0)
    return pl.pallas_call(
        functools.partial(_lru_kernel, L=L, has_ctx=has_ctx),
        grid=(B // BB, LRU_BLOCKS),
        in_specs=in_specs,
        out_specs=[
            pl.BlockSpec((BB, L, W), lambda b, n: (b, 0, n)),
            pl.BlockSpec((BB, 2, W), lambda b, n: (b, 0, n)),
        ],
        out_shape=[
            jax.ShapeDtypeStruct((B, L, D), F32),
            jax.ShapeDtypeStruct((B, 2, D), F32),
        ],
        scratch_shapes=[pltpu.VMEM((BB, L, W), F32)] * 4,
        compiler_params=_cparams(("arbitrary", "arbitrary")),
    )(*args)


def _att_kernel(*refs, L, Q, NB, has_ctx):
    if NB == 1:
        _att_one(*refs, L=L, Q=Q, has_ctx=has_ctx)
    else:
        for bb in range(NB):
            _att_one(refs[0], *[r.at[bb] for r in refs[1:]], L=L, Q=Q, has_ctx=has_ctx)


def _att_one(*refs, L, Q, has_ctx):
    if has_ctx:
        (sink_ref, q_ref, ag_ref, k_ref, v_ref, ck_ref, cv_ref, cosq_ref, sinq_ref, cosk_ref, sink_tab_ref,
         o_ref, km, vm, ckm, cvm) = refs
    else:
        (sink_ref, q_ref, ag_ref, k_ref, v_ref, o_ref, km, vm) = refs
    G = ATT_GROUP
    KW = ATT_KV_W
    qi = pl.program_id(1)

    def head_masked(dst, val):
        head = lax.shift_right_logical(lax.broadcasted_iota(jnp.int32, val.shape, 1), 6)
        for h in range(ATT_KV_HEADS):
            dst[h] = jnp.where(head == h, val, 0.0).astype(BF16)

    @pl.when(qi == 0)
    def _():
        kk = k_ref[...]
        if has_ctx:
            kk = _rope(kk, jnp.concatenate([cosk_ref[...]] * 2, axis=1),
                       jnp.concatenate([sink_tab_ref[...]] * 2, axis=1), 16)
            head_masked(ckm, ck_ref[...])
            head_masked(cvm, cv_ref[...])
        head_masked(km, kk)
        head_masked(vm, v_ref[...])

    q = q_ref[...] * (ATT_HD ** -0.5)
    if has_ctx:
        q = _rope(q, jnp.concatenate([cosq_ref[...]] * 8, axis=1),
                  jnp.concatenate([sinq_ref[...]] * 8, axis=1), 16)
        span = Q + 2 * WINDOW
        start = pl.multiple_of(jnp.clip((qi - 1) * Q, 0, L - span), Q)
        qpos = qi * Q + lax.broadcasted_iota(jnp.int32, (Q, span), 0)
        kpos = start + lax.broadcasted_iota(jnp.int32, (Q, span), 1)
        band = jnp.abs(qpos - kpos) <= WINDOW
        band4 = jnp.concatenate([band] * G, axis=0)
    qg = jnp.concatenate([q[:, g * KW:(g + 1) * KW] for g in range(G)], axis=0).astype(BF16)

    o = None
    for h in range(ATT_KV_HEADS):
        sk = jnp.concatenate(
            [jnp.full((Q, 1), sink_ref[h * G + g], F32) for g in range(G)], axis=0)
        if has_ctx:
            s = jnp.concatenate([jnp.where(band4, _dot_nt(qg, km[h, pl.ds(start, span), :]), -jnp.inf),
                                 _dot_nt(qg, ckm[h])], axis=1)
            m = jnp.maximum(jnp.max(s, axis=-1, keepdims=True), sk)
            p = jnp.exp(s - m)
            den = jnp.sum(p, axis=-1, keepdims=True) + jnp.exp(sk - m)
            pb = p.astype(BF16)
            oh = _dot(pb[:, :span], vm[h, pl.ds(start, span), :]) + _dot(pb[:, span:], cvm[h])
        else:
            s1 = _dot_nt(qg, km[h])
            m = jnp.maximum(jnp.max(s1, axis=-1, keepdims=True), sk)
            p1 = jnp.exp(s1 - m)
            den = jnp.sum(p1, axis=-1, keepdims=True) + jnp.exp(sk - m)
            oh = _dot(p1.astype(BF16), vm[h])
        oh = oh * (1.0 / den)
        o = oh if o is None else o + oh
    ao = jnp.concatenate([o[g * Q:(g + 1) * Q] for g in range(G)], axis=1)
    o_ref[...] = ao * _silu(ag_ref[...])


def _attention(pm3, sink, ropes, ck, cv, layer=0):
    B, L, _ = pm3.shape
    has_ctx = ck is not None
    Q = CHUNK if has_ctx else L
    NB = 1 if has_ctx else 2
    nb = None if NB == 1 else NB
    lead = () if NB == 1 else (NB,)
    KW = ATT_KV_W
    in_specs = [
        pl.BlockSpec(memory_space=pltpu.SMEM),
        pl.BlockSpec((nb, Q, D), lambda b, i: (b, i, OFF_AQ // D)),
        pl.BlockSpec((nb, Q, D), lambda b, i: (b, i, OFF_AG // D)),
        pl.BlockSpec((nb, L, KW), lambda b, i: (b, 0, OFF_AK // KW)),
        pl.BlockSpec((nb, L, KW), lambda b, i: (b, 0, OFF_AV // KW)),
    ]
    args = [sink, pm3, pm3, pm3, pm3]
    scratch = [pltpu.VMEM(lead + (ATT_KV_HEADS, L, KW), BF16), pltpu.VMEM(lead + (ATT_KV_HEADS, L, KW), BF16)]
    if has_ctx:
        P = ck.shape[2]
        in_specs += [
            pl.BlockSpec((None, None, P, KW), lambda b, i: (b, layer, 0, 0)),
            pl.BlockSpec((None, None, P, KW), lambda b, i: (b, layer, 0, 0)),
            pl.BlockSpec((Q, 128), lambda b, i: (i, 0)),
            pl.BlockSpec((Q, 128), lambda b, i: (i, 0)),
            pl.BlockSpec((L, 128), lambda b, i: (0, 0)),
            pl.BlockSpec((L, 128), lambda b, i: (0, 0)),
        ]
        args += [ck, cv, ropes[0], ropes[1], ropes[0], ropes[1]]
        scratch += [pltpu.VMEM((ATT_KV_HEADS, P, KW), BF16), pltpu.VMEM((ATT_KV_HEADS, P, KW), BF16)]
    return pl.pallas_call(
        functools.partial(_att_kernel, L=L, Q=Q, NB=NB, has_ctx=has_ctx),
        grid=(B // NB, L // Q),
        in_specs=in_specs,
        out_specs=pl.BlockSpec((nb, Q, D), lambda b, i: (b, i, 0)),
        out_shape=jax.ShapeDtypeStruct((B, L, D), F32),
        scratch_shapes=scratch,
        compiler_params=_cparams(("arbitrary", "arbitrary")),
    )(*args)


OUT_TM = 512


def _out_kernel(x_ref, ro_ref, lo_ref, ao_ref, mod_ref, gpre_ref, gpost_ref,
                wm_ref, bm_ref, wb_ref, wo_ref, y_ref):
    x = x_ref[...]
    mod = mod_ref[...]
    h = _modulated_norm(x, gpre_ref[...], mod).astype(BF16)
    merged = None
    for n, br_ref in enumerate((ro_ref, lo_ref, ao_ref)):
        gate = _sigmoid(_dot(h, wm_ref[:, n * D:(n + 1) * D]) + bm_ref[:, n * D:(n + 1) * D])
        z = gate * _dot(br_ref[...].astype(BF16), wb_ref[n])
        merged = z if merged is None else merged + z
    out = _dot(merged.astype(BF16), wo_ref[...])
    nrm = out * lax.rsqrt(jnp.mean(out * out, axis=-1, keepdims=True) + EPS) * gpost_ref[...]
    y_ref[...] = x + mod[:, 2 * D:3 * D] * nrm


def _merge_out(x2d, ro, lo, ao, mod3, row_fn, g_pre, g_post, w_merge, b_merge, w_branch, w_out):
    T = x2d.shape[0]
    tm = OUT_TM
    tok = lambda i: (i, 0)
    once = pl.Buffered(1)
    return pl.pallas_call(
        _out_kernel,
        grid=(T // tm,),
        in_specs=[
            pl.BlockSpec((tm, D), tok),
            pl.BlockSpec((tm, D), tok),
            pl.BlockSpec((tm, D), tok),
            pl.BlockSpec((tm, D), tok),
            pl.BlockSpec((None, 1, 3 * D), lambda i: (row_fn(i * tm), 0, 0)),
            pl.BlockSpec((1, D), lambda i: (0, 0)),
            pl.BlockSpec((1, D), lambda i: (0, 0)),
            pl.BlockSpec((D, N_BRANCH * D), lambda i: (0, 0), pipeline_mode=once),
            pl.BlockSpec((1, N_BRANCH * D), lambda i: (0, 0)),
            pl.BlockSpec((N_BRANCH, D, D), lambda i: (0, 0, 0), pipeline_mode=once),
            pl.BlockSpec((D, D), lambda i: (0, 0), pipeline_mode=once),
        ],
        out_specs=pl.BlockSpec((tm, D), tok),
        out_shape=jax.ShapeDtypeStruct((T, D), F32),
        compiler_params=_cparams(("arbitrary",)),
    )(x2d, ro, lo, ao, mod3, g_pre, g_post, w_merge, b_merge, w_branch, w_out)


def _layer(x, mod3, row_of_token, lw, ctx, ropes_ret, ropes_att):
    B, L, _ = x.shape
    x2d = x.reshape(B * L, D)
    pm2 = _in_proj(x2d, mod3, row_of_token, lw['g_pre'], lw['w_in'])
    pm3 = pm2.reshape(B, L, W_IN)
    if ctx is None:
        ro, ret_s = _retention(pm3, lw['rd4'], None, None)
        lo, lru_s = _rglru(pm3, lw['conv_w'], lw['conv_b'], lw['w_gate'], lw['b_gate'], lw['lam'], None)
        ao = _attention(pm3, lw['sink'], None, None, None)
    else:
        s_ret, s_lru, ck, cv, layer = ctx
        ro, ret_s = _retention(pm3, lw['rd4'], ropes_ret, s_ret, layer)
        lo, lru_s = _rglru(pm3, lw['conv_w'], lw['conv_b'], lw['w_gate'], lw['b_gate'], lw['lam'], s_lru, layer)
        ao = _attention(pm3, lw['sink'], ropes_att, ck, cv, layer)
    y = _merge_out(x2d, ro.reshape(B * L, D), lo.reshape(B * L, D), ao.reshape(B * L, D),
                   mod3, row_of_token, lw['g_pre'], lw['g_post'], lw['w_merge'], lw['b_merge'],
                   lw['w_branch'], lw['w_out'])
    return y.reshape(B, L, D), pm3, ret_s, lru_s


def _group_major(w, axis):
    shp = w.shape
    w = w.reshape(shp[:axis] + (ATT_KV_HEADS, ATT_GROUP, ATT_HD) + shp[axis + 1:])
    w = jnp.swapaxes(w, axis, axis + 1)
    return w.reshape(shp)


def _layer_weights(l, norm_pre, norm_post, w_in, ret_decay, conv_w, conv_b, lru_wa, lru_ba, lru_wx,
                   lru_bx, lru_lambda, att_sink, w_branch, w_merge, b_merge, w_out):
    offs = np.cumsum((512, 512, 1024, 1024, 1024, 1024, 1024, 256, 256, 1024))[:-1].tolist()
    rq, rk, rv, rg, lx, lgt, aq, ak, av, ag = jnp.split(w_in[l], offs, axis=-1)
    w_cat = jnp.concatenate([_group_major(aq, 1), _group_major(ag, 1), rv, rg, lx, lgt, rq, rk, ak, av],
                            axis=-1).astype(BF16)
    wb = jnp.stack([w_branch[l, 0], w_branch[l, 1], _group_major(w_branch[l, 2], 0)], axis=0).astype(BF16)
    w_gate = jnp.concatenate([lru_wa[l, 0], lru_wx[l, 0], lru_wa[l, 1], lru_wx[l, 1]], axis=-1).astype(BF16)
    b4 = jnp.stack([lru_ba[l, 0], lru_bx[l, 0], lru_ba[l, 1], lru_bx[l, 1]], axis=0)
    b_gate = b4.reshape(4, LRU_BLOCKS, LRU_BS).transpose(1, 0, 2).reshape(LRU_BLOCKS, 1, 4 * LRU_BS)
    rd4 = jnp.broadcast_to(ret_decay[l][:, :, None, None], (2, RET_HEADS, 1, RET_DV))
    return {
        'g_pre': norm_pre[l].reshape(1, D), 'g_post': norm_post[l].reshape(1, D),
        'w_in': w_cat, 'rd4': rd4,
        'conv_w': conv_w[l], 'conv_b': conv_b[l].reshape(1, D),
        'w_gate': w_gate, 'b_gate': b_gate, 'lam': lru_lambda[l],
        'sink': att_sink[l], 'w_merge': w_merge[l].astype(BF16), 'b_merge': b_merge[l].reshape(1, N_BRANCH * D),
        'w_branch': wb, 'w_out': w_out[l].astype(BF16),
    }


def kernel(x_prompt, x_sample, cache_k, cache_v, state_ret, state_lru, c, c_ctx, w_ada, b_ada, norm_pre, norm_post, w_in, ret_decay, conv_w, conv_b, lru_wa, lru_ba, lru_wx, lru_bx, lru_lambda, att_sink, w_branch, w_merge, b_merge, w_out):
    B, S, _ = x_prompt.shape
    Bd, Ld, _ = x_sample.shape
    P = cache_k.shape[2]
    cond8 = jnp.concatenate([c_ctx[None, :], c, jnp.zeros((8 - 1 - Bd, D), F32)], axis=0)
    mod = _ada_mod(cond8, w_ada, b_ada)
    ropes_ret = _rope_tables(Ld, RET_DK)
    ropes_att = _rope_tables(Ld, ATT_HD)

    yp, ys = x_prompt, x_sample
    new_k, new_v, new_ret, new_lru = [], [], [], []
    for l in range(DEPTH):
        lw = _layer_weights(l, norm_pre, norm_post, w_in, ret_decay, conv_w, conv_b, lru_wa, lru_ba,
                            lru_wx, lru_bx, lru_lambda, att_sink, w_branch, w_merge, b_merge, w_out)
        mod3 = mod[l].reshape(8, 1, 3 * D)
        yp, pm_p, r_s, l_s = _layer(yp, mod3, lambda t: 0, lw, None, None, None)
        new_ret.append(r_s)
        new_lru.append(l_s)
        new_k.append(pm_p[:, :, OFF_AK:OFF_AK + ATT_KV_W].reshape(B, S, ATT_KV_HEADS, ATT_HD))
        new_v.append(pm_p[:, :, OFF_AV:OFF_AV + ATT_KV_W].reshape(B, S, ATT_KV_HEADS, ATT_HD))
        ctx = (state_ret, state_lru, cache_k.reshape(Bd, DEPTH, P, ATT_KV_W),
               cache_v.reshape(Bd, DEPTH, P, ATT_KV_W), l)
        ys, _, _, _ = _layer(ys, mod3, lambda t: 1 + t // Ld, lw, ctx, ropes_ret, ropes_att)
    return (yp, ys, jnp.stack(new_k, axis=1), jnp.stack(new_v, axis=1),
            jnp.stack(new_ret, axis=1), jnp.stack(new_lru, axis=1))
```
